```python
import jax, jax.numpy as jnp
from jax import lax
import numpy as np

D_MODEL = 2048
BATCH = 4
SEQ = 4096
DEPTH = 4

GRID_W = 64
CTX_LEN = 256
N_MIXERS = 2
N_HEADS = 16
N_KV_HEADS = 4
HEAD_DIM = D_MODEL // N_HEADS
QKV_DIM = (N_HEADS + 2 * N_KV_HEADS) * HEAD_DIM
ROPE_THETA = 10000.0
Q_BLOCK = 128
CONV_WIDTH = 3
D_FF = 5632
N_EXPERTS = 8
TOP_K = 2
EPS = 1e-6
N_EVEN = (DEPTH + 1) // 2
N_ODD = DEPTH // 2

kernel_name = "hybrid_flow_backbone_attn_shortconv_moe"


def rms_norm(x, g):
    xf = x.astype(jnp.float32)
    y = xf * lax.rsqrt(jnp.mean(xf * xf, axis=-1, keepdims=True) + EPS)
    return (y * g.astype(jnp.float32)).astype(x.dtype)


def modulate(x, shift, scale):
    return x * (1 + scale) + shift


def axial_rope_tables(rows):
    r, col = jnp.meshgrid(jnp.arange(rows), jnp.arange(GRID_W), indexing="ij")
    r = r.reshape(-1).astype(jnp.float32)
    col = col.reshape(-1).astype(jnp.float32)
    n_freq = HEAD_DIM // 4
    inv = ROPE_THETA ** (-jnp.arange(n_freq, dtype=jnp.float32) / n_freq)
    ang = jnp.concatenate([r[:, None] * inv, col[:, None] * inv], axis=-1)
    return jnp.cos(ang), jnp.sin(ang)


def apply_rope(x, cos, sin):
    xf = x.astype(jnp.float32).reshape(x.shape[:-1] + (HEAD_DIM // 2, 2))
    x0, x1 = xf[..., 0], xf[..., 1]
    cs = cos[None, :, None, :]
    sn = sin[None, :, None, :]
    out = jnp.stack([x0 * cs - x1 * sn, x0 * sn + x1 * cs], axis=-1).reshape(x.shape)
    return out.astype(x.dtype)


def qkv_heads(h, w_qkv, q_g, k_g):
    b, n, _ = h.shape
    q, k, v = jnp.split(h @ w_qkv, [N_HEADS * HEAD_DIM, (N_HEADS + N_KV_HEADS) * HEAD_DIM], axis=-1)
    q = rms_norm(q.reshape(b, n, N_HEADS, HEAD_DIM), q_g)
    k = rms_norm(k.reshape(b, n, N_KV_HEADS, HEAD_DIM), k_g)
    v = v.reshape(b, n, N_KV_HEADS, HEAD_DIM)
    return q, k, v


def gqa_attend(q, k, v):
    s = jnp.einsum("bqkgd,bskd->bkgqs", q, k, preferred_element_type=jnp.float32) * (HEAD_DIM ** -0.5)
    p = jax.nn.softmax(s, axis=-1).astype(v.dtype)
    return jnp.einsum("bkgqs,bskd->bqkgd", p, v)


def attention_mixer(hc, hl, w_qkv, q_g, k_g, w_o, cos, sin, need_ctx):
    b, s, _ = hl.shape
    c_len = hc.shape[1]
    grp = N_HEADS // N_KV_HEADS
    qc, kc, vc = qkv_heads(hc, w_qkv, q_g, k_g)
    ql, kl, vl = qkv_heads(hl, w_qkv, q_g, k_g)
    ql = apply_rope(ql, cos, sin)
    kl = apply_rope(kl, cos, sin)
    k_all = jnp.concatenate([kc, kl], axis=1)
    v_all = jnp.concatenate([vc, vl], axis=1)
    nb = s // Q_BLOCK
    qb = ql.reshape(b, nb, Q_BLOCK, N_KV_HEADS, grp, HEAD_DIM).transpose(1, 0, 2, 3, 4, 5)
    ob = lax.map(lambda q: gqa_attend(q, k_all, v_all), qb)
    ol = ob.transpose(1, 0, 2, 3, 4, 5).reshape(b, s, N_HEADS * HEAD_DIM)
    yl = ol @ w_o
    yc = None
    if need_ctx:
        oc = gqa_attend(qc.reshape(b, c_len, N_KV_HEADS, grp, HEAD_DIM), kc, vc)
        yc = oc.reshape(b, c_len, N_HEADS * HEAD_DIM) @ w_o
    return yc, yl


def short_conv_mixer(h, w_in, conv_w, w_out):
    n = h.shape[1]
    b_gate, c_gate, u = jnp.split(h @ w_in, 3, axis=-1)
    v = c_gate * u
    half = CONV_WIDTH // 2
    vp = jnp.pad(v, ((0, 0), (half, half), (0, 0)))
    y = sum(vp[:, k:k + n] * conv_w[k] for k in range(CONV_WIDTH))
    return (b_gate * y) @ w_out


def swiglu(h, w_gu, w_down):
    g, u = jnp.split(h @ w_gu, 2, axis=-1)
    return (jax.nn.silu(g) * u) @ w_down


def moe_swiglu(h, router, w_gu, w_down):
    logits = jnp.einsum("bnd,de->bne", h, router, preferred_element_type=jnp.float32)
    top_v, top_i = lax.top_k(logits, TOP_K)
    top_w = jax.nn.softmax(top_v, axis=-1)
    gates = jnp.sum(jax.nn.one_hot(top_i, N_EXPERTS, dtype=jnp.float32) * top_w[..., None], axis=-2)
    out = jnp.zeros_like(h)
    for e in range(N_EXPERTS):
        out = out + gates[..., e:e + 1].astype(h.dtype) * swiglu(h, w_gu[e], w_down[e])
    return out


def setup_inputs(seed: int = 0) -> dict:
    key = jax.random.key(seed)
    ks = jax.random.split(key, 19)
    D = D_MODEL

    def nrm(k, shape, scale):
        return jax.random.normal(k, shape, jnp.float32) * scale

    return {
        "x": nrm(ks[0], (BATCH, SEQ, D), 1.0),
        "c": nrm(ks[1], (BATCH, D), 1.0),
        "ctx": nrm(ks[2], (BATCH, CTX_LEN, D), 1.0),
        "c_ctx": nrm(ks[3], (D,), 1.0),
        "ada_w": nrm(ks[4], (DEPTH, D, 6 * D), 0.5 * D ** -0.5),
        "ada_b": nrm(ks[5], (DEPTH, 6 * D), 0.02),
        "norm_g": 1.0 + nrm(ks[6], (DEPTH, 4, D), 0.1),
        "attn_w_qkv": nrm(ks[7], (N_EVEN, D, QKV_DIM), D ** -0.5),
        "attn_q_norm": 1.0 + nrm(ks[8], (N_EVEN, HEAD_DIM), 0.1),
        "attn_k_norm": 1.0 + nrm(ks[9], (N_EVEN, HEAD_DIM), 0.1),
        "attn_w_o": nrm(ks[10], (N_EVEN, N_HEADS * HEAD_DIM, D), (N_HEADS * HEAD_DIM) ** -0.5),
        "conv_w_in": nrm(ks[11], (N_ODD, D, 3 * D), D ** -0.5),
        "conv_w": nrm(ks[12], (N_ODD, CONV_WIDTH, D), CONV_WIDTH ** -0.5),
        "conv_w_out": nrm(ks[13], (N_ODD, D, D), D ** -0.5),
        "ffn_w_gu": nrm(ks[14], (N_EVEN, D, 2 * D_FF), D ** -0.5),
        "ffn_w_down": nrm(ks[15], (N_EVEN, D_FF, D), D_FF ** -0.5),
        "moe_router": nrm(ks[16], (N_ODD, D, N_EXPERTS), D ** -0.5),
        "moe_w_gu": nrm(ks[17], (N_ODD, N_EXPERTS, D, 2 * D_FF), D ** -0.5),
        "moe_w_down": nrm(ks[18], (N_ODD, N_EXPERTS, D_FF, D), D_FF ** -0.5),
    }


def reference(x, c, ctx, c_ctx, ada_w, ada_b, norm_g, attn_w_qkv, attn_q_norm, attn_k_norm, attn_w_o,
              conv_w_in, conv_w, conv_w_out, ffn_w_gu, ffn_w_down, moe_router, moe_w_gu, moe_w_down):
    ROWS = x.shape[1] // GRID_W
    cos, sin = axial_rope_tables(ROWS)
    xl, xc = x, ctx
    silu_c = jax.nn.silu(c)
    silu_cc = jax.nn.silu(c_ctx)
    for i in range(DEPTH):
        j = i // 2
        is_attn = (i % N_MIXERS) == 0
        need_ctx = i < DEPTH - 1
        if i % 2 == 0:
            ffn = lambda h: swiglu(h, ffn_w_gu[j], ffn_w_down[j])
        else:
            ffn = lambda h: moe_swiglu(h, moe_router[j], moe_w_gu[j], moe_w_down[j])

        mod_l = (silu_c @ ada_w[i] + ada_b[i])[:, None, :]
        sh1, sc1, g1, sh2, sc2, g2 = jnp.split(mod_l, 6, axis=-1)
        hl = modulate(rms_norm(xl, norm_g[i, 0]), sh1, sc1)
        if need_ctx or is_attn:
            mod_c = silu_cc @ ada_w[i] + ada_b[i]
            csh1, csc1, cg1, csh2, csc2, cg2 = jnp.split(mod_c, 6, axis=-1)
            hc = modulate(rms_norm(xc, norm_g[i, 0]), csh1, csc1)

        if is_attn:
            yc, yl = attention_mixer(hc, hl, attn_w_qkv[j], attn_q_norm[j], attn_k_norm[j], attn_w_o[j],
                                     cos, sin, need_ctx)
        else:
            yl = short_conv_mixer(hl, conv_w_in[j], conv_w[j], conv_w_out[j])
            yc = short_conv_mixer(hc, conv_w_in[j], conv_w[j], conv_w_out[j]) if need_ctx else None

        xl = xl + g1 * rms_norm(yl, norm_g[i, 1])
        hl = modulate(rms_norm(xl, norm_g[i, 2]), sh2, sc2)
        xl = xl + g2 * rms_norm(ffn(hl), norm_g[i, 3])

        if need_ctx:
            xc = xc + cg1 * rms_norm(yc, norm_g[i, 1])
            hc = modulate(rms_norm(xc, norm_g[i, 2]), csh2, csc2)
            xc = xc + cg2 * rms_norm(ffn(hc), norm_g[i, 3])
    return xl
```

```python
import functools

import jax
import jax.numpy as jnp
from jax import lax
from jax.experimental import pallas as pl
from jax.experimental.pallas import tpu as pltpu

EPS = 1e-6
GRID_W = 64
ROPE_THETA = 10000.0
LANES = 128
SUBLANES = 8
VMEM_LIMIT = 56 * 1024 * 1024

F32 = jnp.float32
BF16 = jnp.bfloat16


def _params(sem, vmem=VMEM_LIMIT):
    return pltpu.CompilerParams(dimension_semantics=sem, vmem_limit_bytes=vmem)


def _pick(n, candidates):
    for c in candidates:
        if n % c == 0:
            return c
    raise ValueError(f"no tile in {candidates} divides {n}")


def _mod_kernel(c_ref, w_ref, b_ref, o_ref):
    c = c_ref[...]
    s = c * (1.0 / (1.0 + jnp.exp(-c)))
    o_ref[...] = jnp.dot(s.astype(BF16), w_ref[...].astype(BF16), preferred_element_type=F32) + b_ref[...]


def _modulation(cvec, ada_w, ada_b):
    depth, d, n = ada_w.shape
    r = cvec.shape[0]
    tn = _pick(n, (1024, 512, 256, 128))
    return pl.pallas_call(
        _mod_kernel,
        grid=(depth, n // tn),
        in_specs=[
            pl.BlockSpec((r, d), lambda l, j: (0, 0)),
            pl.BlockSpec((None, d, tn), lambda l, j: (l, 0, j)),
            pl.BlockSpec((None, 1, tn), lambda l, j: (l, 0, j)),
        ],
        out_specs=pl.BlockSpec((None, r, tn), lambda l, j: (l, 0, j)),
        out_shape=jax.ShapeDtypeStruct((depth, r, n), F32),
        compiler_params=_params(("arbitrary", "arbitrary")),
        name="ada_mod",
    )(cvec, ada_w, ada_b.reshape(depth, 1, n))


def _rms(x, g):
    return x * lax.rsqrt(jnp.mean(x * x, axis=-1, keepdims=True) + EPS) * g


def _resnorm_kernel(*refs, n_y, has_h):
    it = iter(refs)
    x = next(it)[...]
    if n_y:
        ys = [next(it)[...] for _ in range(n_y)]
        if n_y == 2:
            w = next(it)[...]
            y = w[:, 0:1] * ys[0] + w[:, 1:2] * ys[1]
        else:
            y = ys[0]
        g_post = next(it)[...]
        gate = next(it)[...]
        x = x + gate * _rms(y, g_post)
    if has_h:
        g_pre = next(it)[...]
        shift = next(it)[...]
        scale = next(it)[...]
    if n_y:
        next(it)[...] = x
    if has_h:
        h_ref = next(it)
        h_ref[...] = (_rms(x, g_pre) * (1.0 + scale) + shift).astype(h_ref.dtype)


def _resnorm(cfg, x, mod, *, y=None, wts=None, g_post=None, gate_idx=None, g_pre=None,
             shift_idx=None, scale_idx=None, h_dtype=BF16):
    n, d = x.shape
    tm = cfg["tm_rows"]
    tiles_per_batch = cfg["T"] // tm
    ctx_tiles = cfg["CTX"] // tm
    nb, r = cfg["B"], cfg["R"]
    n_y = 0 if y is None else (1 if wts is None else 2)
    has_h = g_pre is not None

    def mod_spec(idx):
        l, k = idx

        def imap(i):
            row = jnp.where(i % tiles_per_batch < ctx_tiles, nb, i // tiles_per_batch)
            return ((l * 6 + k) * r + row, 0, 0)

        return pl.BlockSpec((None, 1, d), imap)

    row_spec = pl.BlockSpec((tm, d), lambda i: (i, 0))
    vec_spec = pl.BlockSpec((1, d), lambda i: (0, 0))
    args, specs = [x], [row_spec]
    if n_y:
        if n_y == 2:
            half = n // tm
            args += [y, y, wts]
            specs += [row_spec, pl.BlockSpec((tm, d), lambda i: (i + half, 0)),
                      pl.BlockSpec((tm, LANES), lambda i: (i, 0))]
        else:
            args += [y]
            specs += [row_spec]
        args += [g_post.reshape(1, d), mod]
        specs += [vec_spec, mod_spec(gate_idx)]
    if has_h:
        args += [g_pre.reshape(1, d), mod, mod]
        specs += [vec_spec, mod_spec(shift_idx), mod_spec(scale_idx)]
    out_shape, out_specs = [], []
    if n_y:
        out_shape.append(jax.ShapeDtypeStruct((n, d), F32))
        out_specs.append(row_spec)
    if has_h:
        out_shape.append(jax.ShapeDtypeStruct((n, d), h_dtype))
        out_specs.append(row_spec)
    return pl.pallas_call(
        functools.partial(_resnorm_kernel, n_y=n_y, has_h=has_h),
        grid=(n // tm,),
        in_specs=specs,
        out_specs=out_specs,
        out_shape=out_shape,
        compiler_params=_params(("arbitrary",)),
        name="resnorm",
    )(*args)


def _mm_plain_kernel(te_ref, na_ref, a_ref, w_ref, o_ref):
    i = pl.program_id(1)

    @pl.when(i < na_ref[0])
    def _():
        o_ref[...] = jnp.dot(a_ref[...].astype(BF16), w_ref[...],
                             preferred_element_type=F32).astype(o_ref.dtype)

    @pl.when(i >= na_ref[0])
    def _():
        o_ref[...] = jnp.zeros_like(o_ref)


def _mm_swiglu_kernel(te_ref, na_ref, a_ref, wg_ref, wu_ref, o_ref):
    i = pl.program_id(1)

    @pl.when(i < na_ref[0])
    def _():
        a = a_ref[...].astype(BF16)
        g = jnp.dot(a, wg_ref[...], preferred_element_type=F32)
        u = jnp.dot(a, wu_ref[...], preferred_element_type=F32)
        o_ref[...] = (g * (1.0 / (1.0 + jnp.exp(-g))) * u).astype(o_ref.dtype)

    @pl.when(i >= na_ref[0])
    def _():
        o_ref[...] = jnp.zeros_like(o_ref)


def _grouped_matmul(a, w, tile_expert, n_active, *, tm, tn, swiglu=False, out_dtype=F32, name="mm"):
    m, k = a.shape
    n_out = w.shape[2] // 2 if swiglu else w.shape[2]
    nj, ni = n_out // tn, m // tm

    def a_map(j, i, te, na):
        return (jnp.minimum(i, na[0] - 1), 0)

    def w_map(off):
        return lambda j, i, te, na: (te[i], 0, j + off)

    w_spec = lambda off: pl.BlockSpec((None, k, tn), w_map(off))
    in_specs = [pl.BlockSpec((tm, k), a_map), w_spec(0)]
    args = [a, w]
    if swiglu:
        in_specs.append(w_spec(nj))
        args.append(w)
    return pl.pallas_call(
        _mm_swiglu_kernel if swiglu else _mm_plain_kernel,
        grid_spec=pltpu.PrefetchScalarGridSpec(
            num_scalar_prefetch=2,
            grid=(nj, ni),
            in_specs=in_specs,
            out_specs=pl.BlockSpec((tm, tn), lambda j, i, te, na: (i, j)),
        ),
        out_shape=jax.ShapeDtypeStruct((m, n_out), out_dtype),
        compiler_params=_params(("arbitrary", "arbitrary")),
        name=name,
    )(tile_expert, n_active, *args)


def _dense_matmul(a, w, *, tm, tn, swiglu=False, out_dtype=F32, name="mm"):
    ni = a.shape[0] // tm
    return _grouped_matmul(a, w[None], jnp.zeros((ni,), jnp.int32), jnp.full((1,), ni, jnp.int32),
                           tm=tm, tn=tn, swiglu=swiglu, out_dtype=out_dtype, name=name)


def _qkv_kernel(a_ref, w_ref, g_ref, cos_ref, sin_ref, o_ref, *, n_norm_tiles, hd):
    j = pl.program_id(0)
    acc = jnp.dot(a_ref[...], w_ref[...], preferred_element_type=F32)

    @pl.when(j < n_norm_tiles)
    def _():
        cos = cos_ref[...]
        sin = sin_ref[...]
        even = (lax.broadcasted_iota(jnp.int32, cos.shape, 1) % 2) == 0
        for c in range(acc.shape[1] // hd):
            x = acc[:, c * hd:(c + 1) * hd]
            x = x * lax.rsqrt(jnp.mean(x * x, axis=-1, keepdims=True) + EPS) * g_ref[:, c * hd:(c + 1) * hd]
            partner = jnp.where(even, pltpu.roll(x, hd - 1, axis=1), pltpu.roll(x, 1, axis=1))
            o_ref[:, c * hd:(c + 1) * hd] = (x * cos + partner * sin).astype(o_ref.dtype)

    @pl.when(j >= n_norm_tiles)
    def _():
        o_ref[...] = acc.astype(o_ref.dtype)


def _qkv_proj(cfg, h, w_qkv, gain, cos_t, sin_t):
    n, d = h.shape
    nq = w_qkv.shape[1]
    hd, nh, kvh = cfg["hd"], cfg["NH"], cfg["KVH"]
    tm = cfg["tm_rows"]
    tn = _pick(kvh * hd, (512, 256, 128))
    tiles_per_batch = cfg["T"] // tm
    return pl.pallas_call(
        functools.partial(_qkv_kernel, n_norm_tiles=(nh + kvh) * hd // tn, hd=hd),
        grid=(nq // tn, n // tm),
        in_specs=[
            pl.BlockSpec((tm, d), lambda j, i: (i, 0)),
            pl.BlockSpec((d, tn), lambda j, i: (0, j)),
            pl.BlockSpec((1, tn), lambda j, i: (0, j)),
            pl.BlockSpec((tm, hd), lambda j, i: (i % tiles_per_batch, 0)),
            pl.BlockSpec((tm, hd), lambda j, i: (i % tiles_per_batch, 0)),
        ],
        out_specs=pl.BlockSpec((tm, tn), lambda j, i: (i, j)),
        out_shape=jax.ShapeDtypeStruct((n, nq), BF16),
        compiler_params=_params(("arbitrary", "arbitrary")),
        name="qkv_proj",
    )(h, w_qkv, gain, cos_t, sin_t)


def _attn_kernel(q_ref, k_ref, v_ref, o_ref, s_ref, *, grp, hd, ctx_len, ctx_tiles, kv_chunk):
    t = pl.program_id(2)
    tq = q_ref.shape[0]
    q = jnp.concatenate([q_ref[:, g * hd:(g + 1) * hd] for g in range(grp)], axis=0)

    def attend(kv_len):
        chunks = [(s, min(kv_chunk, kv_len - s)) for s in range(0, kv_len, kv_chunk)]
        m = None
        for s0, sz in chunks:
            s = lax.dot_general(q, k_ref[s0:s0 + sz, :], (((1,), (1,)), ((), ())),
                                preferred_element_type=F32)
            s_ref[:, s0:s0 + sz] = s
            cm = jnp.max(s, axis=-1, keepdims=True)
            m = cm if m is None else jnp.maximum(m, cm)
        l = jnp.zeros_like(m)
        acc = jnp.zeros((grp * tq, hd), F32)
        for s0, sz in chunks:
            p = jnp.exp(s_ref[:, s0:s0 + sz] - m)
            l = l + jnp.sum(p, axis=-1, keepdims=True)
            acc = acc + jnp.dot(p.astype(BF16), v_ref[s0:s0 + sz, :], preferred_element_type=F32)
        o = acc * (1.0 / l)
        for g in range(grp):
            o_ref[:, g * hd:(g + 1) * hd] = o[g * tq:(g + 1) * tq].astype(o_ref.dtype)

    pl.when(t < ctx_tiles)(lambda: attend(ctx_len))
    pl.when(t >= ctx_tiles)(lambda: attend(k_ref.shape[0]))


def _attention(cfg, qkv):
    n = qkv.shape[0]
    hd, nh, kvh, t_len = cfg["hd"], cfg["NH"], cfg["KVH"], cfg["T"]
    grp = nh // kvh
    tq = cfg["tq"]
    tiles_per_batch = t_len // tq
    return pl.pallas_call(
        functools.partial(_attn_kernel, grp=grp, hd=hd, ctx_len=cfg["CTX"], ctx_tiles=cfg["CTX"] // tq,
                          kv_chunk=1024),
        grid=(cfg["B"], kvh, tiles_per_batch),
        in_specs=[
            pl.BlockSpec((tq, grp * hd), lambda b, kh, t: (b * tiles_per_batch + t, kh)),
            pl.BlockSpec((t_len, hd), lambda b, kh, t: (b, nh + kh)),
            pl.BlockSpec((t_len, hd), lambda b, kh, t: (b, nh + kvh + kh)),
        ],
        out_specs=pl.BlockSpec((tq, grp * hd), lambda b, kh, t: (b * tiles_per_batch + t, kh)),
        out_shape=jax.ShapeDtypeStruct((n, nh * hd), BF16),
        scratch_shapes=[pltpu.VMEM((grp * tq, t_len), F32)],
        compiler_params=_params(("arbitrary", "arbitrary", "arbitrary")),
        name="attention",
    )(qkv, qkv, qkv)


def _convgate_kernel(b_ref, c_ref, u_ref, cp_ref, up_ref, cn_ref, un_ref, w_ref, o_ref, *,
                     tiles_per_batch, ctx_tiles):
    i = pl.program_id(0)
    tm = b_ref.shape[0]
    tpos = i % tiles_per_batch
    has_prev = jnp.logical_and(tpos != 0, tpos != ctx_tiles)
    has_next = jnp.logical_and(tpos != ctx_tiles - 1, tpos != tiles_per_batch - 1)
    v = c_ref[...] * u_ref[...]
    v_before = jnp.where(has_prev, cp_ref[SUBLANES - 1:SUBLANES, :] * up_ref[SUBLANES - 1:SUBLANES, :], 0.0)
    v_after = jnp.where(has_next, cn_ref[0:1, :] * un_ref[0:1, :], 0.0)
    row = lax.broadcasted_iota(jnp.int32, v.shape, 0)
    v_prev = jnp.where(row == 0, v_before, pltpu.roll(v, 1, axis=0))
    v_next = jnp.where(row == tm - 1, v_after, pltpu.roll(v, tm - 1, axis=0))
    y = v_prev * w_ref[0:1, :] + v * w_ref[1:2, :] + v_next * w_ref[2:3, :]
    o_ref[...] = (b_ref[...] * y).astype(o_ref.dtype)


def _convgate(cfg, z, conv_w):
    n, d3 = z.shape
    d = d3 // 3
    tm = cfg["tm_rows"]
    sub_per_tile = tm // SUBLANES
    last_sub = n // SUBLANES - 1
    tiles_per_batch = cfg["T"] // tm

    def prev_map(col):
        return lambda i: (jnp.maximum(i * sub_per_tile - 1, 0), col)

    def next_map(col):
        return lambda i: (jnp.minimum((i + 1) * sub_per_tile, last_sub), col)

    return pl.pallas_call(
        functools.partial(_convgate_kernel, tiles_per_batch=tiles_per_batch, ctx_tiles=cfg["CTX"] // tm),
        grid=(n // tm,),
        in_specs=[
            pl.BlockSpec((tm, d), lambda i: (i, 0)),
            pl.BlockSpec((tm, d), lambda i: (i, 1)),
            pl.BlockSpec((tm, d), lambda i: (i, 2)),
            pl.BlockSpec((SUBLANES, d), prev_map(1)),
            pl.BlockSpec((SUBLANES, d), prev_map(2)),
            pl.BlockSpec((SUBLANES, d), next_map(1)),
            pl.BlockSpec((SUBLANES, d), next_map(2)),
            pl.BlockSpec((conv_w.shape[0], d), lambda i: (0, 0)),
        ],
        out_specs=pl.BlockSpec((tm, d), lambda i: (i, 0)),
        out_shape=jax.ShapeDtypeStruct((n, d), BF16),
        compiler_params=_params(("arbitrary",)),
        name="convgate",
    )(z, z, z, z, z, z, z, conv_w)


def _route_kernel(h_ref, r_ref, meta_ref, wts_ref, cnt_ref, carry_ref, *, n_experts):
    i = pl.program_id(0)

    @pl.when(i == 0)
    def _():
        carry_ref[...] = jnp.zeros_like(carry_ref)

    tm = h_ref.shape[0]
    logits = jnp.dot(h_ref[...].astype(BF16), r_ref[...], preferred_element_type=F32)
    lane = lax.broadcasted_iota(jnp.int32, logits.shape, 1)
    neg = jnp.float32(-jnp.inf)
    lg = jnp.where(lane < n_experts, logits, neg)
    v1 = jnp.max(lg, axis=-1, keepdims=True)
    i1 = jnp.min(jnp.where(lg == v1, lane, LANES), axis=-1, keepdims=True)
    lg2 = jnp.where(lane == i1, neg, lg)
    v2 = jnp.max(lg2, axis=-1, keepdims=True)
    i2 = jnp.min(jnp.where(lg2 == v2, lane, LANES), axis=-1, keepdims=True)
    e = jnp.exp(v2 - v1)
    w1 = 1.0 / (1.0 + e)
    w2 = e / (1.0 + e)
    sel1 = lane == i1
    sel2 = lane == i2
    onehot = jnp.where(jnp.logical_or(sel1, sel2), 1.0, 0.0)
    rr = lax.broadcasted_iota(jnp.int32, (tm, tm), 0)
    cc = lax.broadcasted_iota(jnp.int32, (tm, tm), 1)
    tri = jnp.where(rr > cc, 1.0, 0.0).astype(BF16)
    before = jnp.dot(tri, onehot.astype(BF16), preferred_element_type=F32) + carry_ref[...]
    r1 = jnp.sum(jnp.where(sel1, before, 0.0), axis=-1, keepdims=True).astype(jnp.int32)
    r2 = jnp.sum(jnp.where(sel2, before, 0.0), axis=-1, keepdims=True).astype(jnp.int32)
    carry_ref[...] = carry_ref[...] + jnp.sum(onehot, axis=0, keepdims=True)
    meta_ref[...] = jnp.where(lane == 0, i1, jnp.where(lane == 1, i2, jnp.where(lane == 2, r1,
                              jnp.where(lane == 3, r2, 0))))
    wts_ref[...] = jnp.where(lane == 0, w1, jnp.where(lane == 1, w2, 0.0))
    cnt_ref[...] = jnp.broadcast_to(carry_ref[...], cnt_ref.shape)


def _route(cfg, h, router):
    n, d = h.shape
    e = router.shape[1]
    tm = cfg["tm_rows"]
    rpad = jnp.zeros((d, LANES), BF16).at[:, :e].set(router.astype(BF16))
    return pl.pallas_call(
        functools.partial(_route_kernel, n_experts=e),
        grid=(n // tm,),
        in_specs=[pl.BlockSpec((tm, d), lambda i: (i, 0)), pl.BlockSpec((d, LANES), lambda i: (0, 0))],
        out_specs=[pl.BlockSpec((tm, LANES), lambda i: (i, 0)), pl.BlockSpec((tm, LANES), lambda i: (i, 0)),
                   pl.BlockSpec((SUBLANES, LANES), lambda i: (0, 0))],
        out_shape=[jax.ShapeDtypeStruct((n, LANES), jnp.int32), jax.ShapeDtypeStruct((n, LANES), F32),
                   jax.ShapeDtypeStruct((SUBLANES, LANES), F32)],
        scratch_shapes=[pltpu.VMEM((1, LANES), F32)],
        compiler_params=_params(("arbitrary",)),
        name="route",
    )(h, rpad)


def _gather_kernel(idx_ref, src_ref, out_ref, sem, *, chunk, unroll):
    base = pl.program_id(0) * chunk

    def row_copy(src_row, dst_row):
        return pltpu.make_async_copy(src_ref.at[pl.ds(src_row, 1)], out_ref.at[pl.ds(dst_row, 1)], sem)

    def body(o, carry):
        for u in range(unroll):
            r = o * unroll + u
            row_copy(idx_ref[0, r], base + r).start()
        return carry

    lax.fori_loop(0, chunk // unroll, body, 0)

    def wait_body(o, carry):
        for u in range(unroll):
            row_copy(0, base).wait()
        return carry

    lax.fori_loop(0, chunk // unroll, wait_body, 0)


def _gather_rows(src, idx, *, chunk=512):
    p = idx.shape[0]
    d = src.shape[1]
    chunk = _pick(p, (chunk, 256, 128, 64, 32, 16, 8))
    steps = p // chunk
    return pl.pallas_call(
        functools.partial(_gather_kernel, chunk=chunk, unroll=8),
        grid=(steps,),
        in_specs=[pl.BlockSpec((None, 1, chunk), lambda i: (i, 0, 0), memory_space=pltpu.SMEM),
                  pl.BlockSpec(memory_space=pl.ANY)],
        out_specs=pl.BlockSpec(memory_space=pl.ANY),
        out_shape=jax.ShapeDtypeStruct((p, d), src.dtype),
        scratch_shapes=[pltpu.SemaphoreType.DMA(())],
        compiler_params=_params(("arbitrary",)),
        name="gather_rows",
    )(idx.reshape(steps, 1, chunk), src)


def _moe(cfg, h, router, w_gu, w_down):
    n, d = h.shape
    n_exp = router.shape[1]
    tm = cfg["tm_moe"]
    meta, wts, cnt = _route(cfg, h, router)
    counts = cnt[0, :n_exp].astype(jnp.int32)
    tiles_e = (counts + tm - 1) // tm
    tiles_cum = jnp.cumsum(tiles_e)
    row_off = (tiles_cum - tiles_e) * tm
    pos = jnp.concatenate([row_off[meta[:, 0]] + meta[:, 2], row_off[meta[:, 1]] + meta[:, 3]])
    n_tiles = (2 * n) // tm + n_exp
    tok = jnp.arange(n, dtype=jnp.int32)
    src = jnp.zeros((n_tiles * tm,), jnp.int32).at[pos].set(jnp.concatenate([tok, tok]))
    n_active = tiles_cum[-1:]
    tile_ids = jnp.minimum(jnp.arange(n_tiles, dtype=jnp.int32), n_active[0] - 1)
    tile_expert = jnp.minimum(jnp.searchsorted(tiles_cum, tile_ids, side="right"), n_exp - 1).astype(jnp.int32)

    xs = _gather_rows(h, src)
    mid = _grouped_matmul(xs, w_gu, tile_expert, n_active, tm=tm, tn=cfg["tn_ff"], swiglu=True,
                          out_dtype=BF16, name="moe_up")
    out = _grouped_matmul(mid, w_down, tile_expert, n_active, tm=tm, tn=cfg["tn_down"], name="moe_down")
    return _gather_rows(out, pos), wts


def _rope_tables(cfg):
    hd, seq, ctx = cfg["hd"], cfg["SEQ"], cfg["CTX"]
    rows = seq // GRID_W
    r, col = jnp.meshgrid(jnp.arange(rows), jnp.arange(GRID_W), indexing="ij")
    r = r.reshape(-1).astype(F32)
    col = col.reshape(-1).astype(F32)
    n_freq = hd // 4
    inv = ROPE_THETA ** (-jnp.arange(n_freq, dtype=F32) / n_freq)
    ang = jnp.concatenate([r[:, None] * inv, col[:, None] * inv], axis=-1)
    cos = jnp.repeat(jnp.cos(ang), 2, axis=-1)
    sin = jnp.repeat(jnp.sin(ang), 2, axis=-1) * jnp.tile(jnp.array([-1.0, 1.0], F32), hd // 2)
    cos = jnp.concatenate([jnp.ones((ctx, hd), F32), cos], axis=0)
    sin = jnp.concatenate([jnp.zeros((ctx, hd), F32), sin], axis=0)
    return cos, sin


def kernel(x, c, ctx, c_ctx, ada_w, ada_b, norm_g, attn_w_qkv, attn_q_norm, attn_k_norm, attn_w_o,
           conv_w_in, conv_w, conv_w_out, ffn_w_gu, ffn_w_down, moe_router, moe_w_gu, moe_w_down):
    nb, seq, d = x.shape
    ctx_len = ctx.shape[1]
    depth = ada_w.shape[0]
    hd = attn_q_norm.shape[1]
    nh = attn_w_o.shape[1] // hd
    kvh = (attn_w_qkv.shape[2] // hd - nh) // 2
    d_ff = ffn_w_down.shape[1]
    t_len = ctx_len + seq
    tm_rows = _pick(ctx_len, (256, 128))
    cfg = dict(B=nb, SEQ=seq, CTX=ctx_len, T=t_len, hd=hd, NH=nh, KVH=kvh, R=-(-(nb + 1) // SUBLANES) * SUBLANES,
               tm_rows=tm_rows, tq=_pick(ctx_len, (128,)),
               tm_mm=_pick(nb * t_len, (512, 256, 128)), tm_moe=_pick(2 * nb * t_len, (512, 256, 128)),
               tn_ff=_pick(d_ff, (1408, 512, 256, 128)), tn_down=_pick(d, (512, 256, 128)))
    n = nb * t_len
    tm = cfg["tm_mm"]

    xa = jnp.concatenate([ctx, x], axis=1).reshape(n, d)
    cvec = jnp.zeros((cfg["R"], d), F32).at[:nb].set(c).at[nb].set(c_ctx)
    mod = _modulation(cvec, ada_w, ada_b)
    mod = mod.reshape(depth, cfg["R"], 6, d).transpose(0, 2, 1, 3).reshape(depth * 6 * cfg["R"], 1, d)
    cos_t, sin_t = _rope_tables(cfg)

    w_qkv = attn_w_qkv.astype(BF16)
    w_o = attn_w_o.astype(BF16)
    w_cin = conv_w_in.astype(BF16)
    w_cout = conv_w_out.astype(BF16)
    w_gu = ffn_w_gu.astype(BF16)
    w_dn = ffn_w_down.astype(BF16)
    m_gu = moe_w_gu.astype(BF16)
    m_dn = moe_w_down.astype(BF16)

    (h,) = _resnorm(cfg, xa, mod, g_pre=norm_g[0, 0], shift_idx=(0, 0), scale_idx=(0, 1))
    for i in range(depth):
        j = i // 2
        moe_layer = i % 2 == 1
        if i % 2 == 0:
            gain = jnp.concatenate([jnp.tile(attn_q_norm[j] * (hd ** -0.5), nh), jnp.tile(attn_k_norm[j], kvh),
                                    jnp.ones((kvh * hd,), F32)]).reshape(1, -1)
            qkv = _qkv_proj(cfg, h, w_qkv[j], gain, cos_t, sin_t)
            o = _attention(cfg, qkv)
            y = _dense_matmul(o, w_o[j], tm=tm, tn=cfg["tn_down"], name="attn_out")
        else:
            z = _dense_matmul(h, w_cin[j], tm=tm, tn=_pick(3 * d, (1024, 512, 256, 128)), name="conv_in")
            gated = _convgate(cfg, z, conv_w[j])
            y = _dense_matmul(gated, w_cout[j], tm=tm, tn=cfg["tn_down"], name="conv_out")
        xa, h = _resnorm(cfg, xa, mod, y=y, g_post=norm_g[i, 1], gate_idx=(i, 2), g_pre=norm_g[i, 2],
                         shift_idx=(i, 3), scale_idx=(i, 4), h_dtype=F32 if moe_layer else BF16)
        last = i == depth - 1
        nxt = {} if last else dict(g_pre=norm_g[i + 1, 0], shift_idx=(i + 1, 0), scale_idx=(i + 1, 1))
        if moe_layer:
            og, wts = _moe(cfg, h, moe_router[j], m_gu[j], m_dn[j])
            res = _resnorm(cfg, xa, mod, y=og, wts=wts, g_post=norm_g[i, 3], gate_idx=(i, 5), **nxt)
        else:
            mid = _dense_matmul(h, w_gu[j], tm=tm, tn=cfg["tn_ff"], swiglu=True, out_dtype=BF16, name="ffn_up")
            y = _dense_matmul(mid, w_dn[j], tm=tm, tn=cfg["tn_down"], name="ffn_down")
            res = _resnorm(cfg, xa, mod, y=y, g_post=norm_g[i, 3], gate_idx=(i, 5), **nxt)
        xa = res[0]
        if not last:
            h = res[1]
    return xa.reshape(nb, t_len, d)[:, ctx_len:, :]
```

```python
import functools

import jax
import jax.numpy as jnp
from jax import lax
from jax.experimental import pallas as pl
from jax.experimental.pallas import tpu as pltpu

EPS = 1e-6
GRID_W = 64
ROPE_THETA = 10000.0
LANES = 128
SUBLANES = 8
VMEM_LIMIT = 56 * 1024 * 1024

F32 = jnp.float32
BF16 = jnp.bfloat16


def _params(sem, vmem=VMEM_LIMIT):
    return pltpu.CompilerParams(dimension_semantics=sem, vmem_limit_bytes=vmem)


def _pick(n, candidates):
    for c in candidates:
        if n % c == 0:
            return c
    raise ValueError(f"no tile in {candidates} divides {n}")


def _mod_kernel(c_ref, w_ref, b_ref, o_ref):
    c = c_ref[...]
    s = c * (1.0 / (1.0 + jnp.exp(-c)))
    o_ref[...] = jnp.dot(s.astype(BF16), w_ref[...].astype(BF16), preferred_element_type=F32) + b_ref[...]


def _modulation(cvec, ada_w, ada_b):
    depth, d, n = ada_w.shape
    r = cvec.shape[0]
    tn = _pick(n, (1024, 512, 256, 128))
    return pl.pallas_call(
        _mod_kernel,
        grid=(depth, n // tn),
        in_specs=[
            pl.BlockSpec((r, d), lambda l, j: (0, 0)),
            pl.BlockSpec((None, d, tn), lambda l, j: (l, 0, j)),
            pl.BlockSpec((None, 1, tn), lambda l, j: (l, 0, j)),
        ],
        out_specs=pl.BlockSpec((None, r, tn), lambda l, j: (l, 0, j)),
        out_shape=jax.ShapeDtypeStruct((depth, r, n), F32),
        compiler_params=_params(("arbitrary", "arbitrary")),
        name="ada_mod",
    )(cvec, ada_w, ada_b.reshape(depth, 1, n))


def _rms(x, g):
    return x * lax.rsqrt(jnp.mean(x * x, axis=-1, keepdims=True) + EPS) * g


def _resnorm_kernel(*refs, n_y, has_h, slab):
    it = iter(refs)
    x = next(it)[...]
    if n_y:
        y = next(it)[...]
        g_post = next(it)[...]
        gate = next(it)[...]
        x = x + gate * _rms(y, g_post)
    if has_h:
        g_pre = next(it)[...]
        shift = next(it)[...]
        scale = next(it)[...]
        if slab:
            router_ref = next(it)
    if n_y:
        next(it)[...] = x
    if has_h:
        h = _rms(x, g_pre) * (1.0 + scale) + shift
        if slab:
            next(it)[...] = jnp.dot(h, router_ref[...], preferred_element_type=F32, precision=lax.Precision.HIGHEST)
            _store_slabs(next(it), h)
        else:
            h_ref = next(it)
            h_ref[...] = h.astype(h_ref.dtype)


def _store_slabs(slab_ref, v):
    rows, d = v.shape
    nsub = d // LANES
    for s in range(nsub):
        slab_ref[pl.ds(s, rows, stride=nsub), :] = v[:, s * LANES:(s + 1) * LANES]


def _resnorm(cfg, x, mod, *, y=None, g_post=None, gate_idx=None, g_pre=None,
             shift_idx=None, scale_idx=None, router=None, latent_only=False):
    d = x.shape[1]
    tm = cfg["tm_rows"]
    tiles_per_batch = cfg["T"] // tm
    ctx_tiles = cfg["CTX"] // tm
    nb, r = cfg["B"], cfg["R"]
    n_y = 0 if y is None else 1
    has_h = g_pre is not None
    slab = router is not None
    out_tiles_per_batch = tiles_per_batch - ctx_tiles if latent_only else tiles_per_batch
    n = nb * out_tiles_per_batch * tm

    def in_tile(i):
        if latent_only:
            return (i // out_tiles_per_batch) * tiles_per_batch + ctx_tiles + i % out_tiles_per_batch
        return i

    def mod_spec(idx):
        l, k = idx

        def imap(i):
            it = in_tile(i)
            row = jnp.where(it % tiles_per_batch < ctx_tiles, nb, it // tiles_per_batch)
            return ((l * 6 + k) * r + row, 0, 0)

        return pl.BlockSpec((None, 1, d), imap)

    row_spec = pl.BlockSpec((tm, d), lambda i: (i, 0))
    in_row_spec = pl.BlockSpec((tm, d), lambda i: (in_tile(i), 0))
    vec_spec = pl.BlockSpec((1, d), lambda i: (0, 0))
    args, specs = [x], [in_row_spec]
    if n_y:
        args += [y, g_post.reshape(1, d), mod]
        specs += [in_row_spec, vec_spec, mod_spec(gate_idx)]
    if has_h:
        args += [g_pre.reshape(1, d), mod, mod]
        specs += [vec_spec, mod_spec(shift_idx), mod_spec(scale_idx)]
        if slab:
            args.append(router)
            specs.append(pl.BlockSpec((d, LANES), lambda i: (0, 0)))
    out_shape, out_specs = [], []
    if n_y:
        out_shape.append(jax.ShapeDtypeStruct((n, d), F32))
        out_specs.append(row_spec)
    if has_h and slab:
        nsub = d // LANES
        out_shape += [jax.ShapeDtypeStruct((n, LANES), F32), jax.ShapeDtypeStruct((n * nsub, LANES), F32)]
        out_specs += [pl.BlockSpec((tm, LANES), lambda i: (i, 0)), pl.BlockSpec((tm * nsub, LANES), lambda i: (i, 0))]
    elif has_h:
        out_shape.append(jax.ShapeDtypeStruct((n, d), BF16))
        out_specs.append(row_spec)
    return pl.pallas_call(
        functools.partial(_resnorm_kernel, n_y=n_y, has_h=has_h, slab=slab),
        grid=(n // tm,),
        in_specs=specs,
        out_specs=out_specs,
        out_shape=out_shape,
        compiler_params=_params(("arbitrary",)),
        name="resnorm",
    )(*args)


def _mm_plain_kernel(te_ref, na_ref, a_ref, w_ref, o_ref, *, slab):
    i = pl.program_id(1)

    @pl.when(i < na_ref[0])
    def _():
        acc = jnp.dot(a_ref[...], w_ref[...], preferred_element_type=F32)
        if slab:
            _store_slabs(o_ref, acc)
        else:
            o_ref[...] = acc.astype(o_ref.dtype)

    @pl.when(i >= na_ref[0])
    def _():
        o_ref[...] = jnp.zeros_like(o_ref)


def _mm_swiglu_kernel(te_ref, na_ref, a_ref, wg_ref, wu_ref, o_ref):
    i = pl.program_id(1)

    @pl.when(i < na_ref[0])
    def _():
        a = a_ref[...]
        g = jnp.dot(a, wg_ref[...], preferred_element_type=F32)
        u = jnp.dot(a, wu_ref[...], preferred_element_type=F32)
        o_ref[...] = (g * (1.0 / (1.0 + jnp.exp(-g))) * u).astype(o_ref.dtype)

    @pl.when(i >= na_ref[0])
    def _():
        o_ref[...] = jnp.zeros_like(o_ref)


def _grouped_matmul(a, w, tile_expert, n_active, *, tm, tn, swiglu=False, slab=False, out_dtype=F32, name="mm"):
    m, k = a.shape
    n_out = w.shape[2] // 2 if swiglu else w.shape[2]
    nj, ni = n_out // tn, m // tm

    def a_map(j, i, te, na):
        return (jnp.minimum(i, na[0] - 1), 0)

    def w_map(off):
        return lambda j, i, te, na: (te[i], 0, j + off)

    w_mode = dict(pipeline_mode=pl.Buffered(1)) if slab else {}
    w_spec = lambda off: pl.BlockSpec((None, k, tn), w_map(off), **w_mode)
    in_specs = [pl.BlockSpec((tm, k), a_map), w_spec(0)]
    args = [a, w]
    if swiglu:
        in_specs.append(w_spec(nj))
        args.append(w)
        body = _mm_swiglu_kernel
    else:
        body = functools.partial(_mm_plain_kernel, slab=slab)
    if slab:
        assert nj == 1 and out_dtype == F32
        nsub = n_out // LANES
        out_shape = jax.ShapeDtypeStruct((m * nsub, LANES), F32)
        out_spec = pl.BlockSpec((tm * nsub, LANES), lambda j, i, te, na: (i, 0))
    else:
        out_shape = jax.ShapeDtypeStruct((m, n_out), out_dtype)
        out_spec = pl.BlockSpec((tm, tn), lambda j, i, te, na: (i, j))
    return pl.pallas_call(
        body,
        grid_spec=pltpu.PrefetchScalarGridSpec(
            num_scalar_prefetch=2,
            grid=(nj, ni),
            in_specs=in_specs,
            out_specs=out_spec,
        ),
        out_shape=out_shape,
        compiler_params=_params(("arbitrary", "arbitrary")),
        name=name,
    )(tile_expert, n_active, *args)


def _dense_matmul(a, w, *, tm, tn, swiglu=False, out_dtype=F32, name="mm"):
    ni = a.shape[0] // tm
    return _grouped_matmul(a, w[None], jnp.zeros((ni,), jnp.int32), jnp.full((1,), ni, jnp.int32),
                           tm=tm, tn=tn, swiglu=swiglu, out_dtype=out_dtype, name=name)


def _qkv_kernel(a_ref, w_ref, g_ref, cos_ref, sin_ref, o_ref, *, n_norm_tiles, hd, sub):
    j = pl.program_id(0)
    tm = a_ref.shape[0]

    def run(norm):
        for r0 in range(0, tm, sub):
            acc = jnp.dot(a_ref[r0:r0 + sub, :], w_ref[...], preferred_element_type=F32)
            if not norm:
                o_ref[r0:r0 + sub, :] = acc.astype(o_ref.dtype)
                continue
            cos = cos_ref[r0:r0 + sub, :]
            sin = sin_ref[r0:r0 + sub, :]
            even = (lax.broadcasted_iota(jnp.int32, cos.shape, 1) % 2) == 0
            for c in range(acc.shape[1] // hd):
                x = acc[:, c * hd:(c + 1) * hd]
                x = x * lax.rsqrt(jnp.mean(x * x, axis=-1, keepdims=True) + EPS) * g_ref[:, c * hd:(c + 1) * hd]
                partner = jnp.where(even, pltpu.roll(x, hd - 1, axis=1), pltpu.roll(x, 1, axis=1))
                o_ref[r0:r0 + sub, c * hd:(c + 1) * hd] = (x * cos + partner * sin).astype(o_ref.dtype)

    pl.when(j < n_norm_tiles)(lambda: run(True))
    pl.when(j >= n_norm_tiles)(lambda: run(False))


def _qkv_proj(cfg, h, w_qkv, gain, cos_t, sin_t):
    n, d = h.shape
    nq = w_qkv.shape[1]
    hd, nh, kvh = cfg["hd"], cfg["NH"], cfg["KVH"]
    tm = cfg["tm_rows"]
    tn = _pick(kvh * hd, (512, 256, 128))
    tiles_per_batch = cfg["T"] // tm
    return pl.pallas_call(
        functools.partial(_qkv_kernel, n_norm_tiles=(nh + kvh) * hd // tn, hd=hd, sub=min(tm, 128)),
        grid=(nq // tn, n // tm),
        in_specs=[
            pl.BlockSpec((tm, d), lambda j, i: (i, 0)),
            pl.BlockSpec((d, tn), lambda j, i: (0, j)),
            pl.BlockSpec((1, tn), lambda j, i: (0, j)),
            pl.BlockSpec((tm, hd), lambda j, i: (i % tiles_per_batch, 0)),
            pl.BlockSpec((tm, hd), lambda j, i: (i % tiles_per_batch, 0)),
        ],
        out_specs=pl.BlockSpec((tm, tn), lambda j, i: (i, j)),
        out_shape=jax.ShapeDtypeStruct((n, nq), BF16),
        compiler_params=_params(("arbitrary", "arbitrary")),
        name="qkv_proj",
    )(h, w_qkv, gain, cos_t, sin_t)


def _attn_kernel(q_ref, k_ref, v_ref, o_ref, s_ref, vx_ref, *, grp, hd, ctx_len, ctx_tiles, kv_chunk):
    t = pl.program_id(2)
    tq = q_ref.shape[0]

    @pl.when(t == 0)
    def _():
        lane = lax.broadcasted_iota(jnp.int32, (v_ref.shape[0], hd), 1)
        vx_ref[:, :hd] = v_ref[...]
        vx_ref[:, hd:] = jnp.where(lane == 0, 1.0, 0.0).astype(vx_ref.dtype)

    def attend(kv_len):
        chunks = [(s, min(kv_chunk, kv_len - s)) for s in range(0, kv_len, kv_chunk)]

        def scores(g):
            q = q_ref[:, g * hd:(g + 1) * hd]
            m = None
            for s0, sz in chunks:
                s = lax.dot_general(q, k_ref[s0:s0 + sz, :], (((1,), (1,)), ((), ())),
                                    preferred_element_type=F32)
                s_ref[g * tq:(g + 1) * tq, s0:s0 + sz] = s
                for c in range(sz // LANES):
                    piece = s[:, c * LANES:(c + 1) * LANES]
                    m = piece if m is None else jnp.maximum(m, piece)
            return jnp.max(m, axis=-1, keepdims=True)

        def mix(g, m):
            acc = jnp.zeros((tq, 2 * hd), F32)
            for s0, sz in chunks:
                p = jnp.exp2(s_ref[g * tq:(g + 1) * tq, s0:s0 + sz] - m)
                acc = acc + jnp.dot(p.astype(BF16), vx_ref[s0:s0 + sz, :], preferred_element_type=F32)
            o = acc[:, :hd] * (1.0 / acc[:, hd:hd + 1])
            o_ref[:, g * hd:(g + 1) * hd] = o.astype(o_ref.dtype)

        row_max = [scores(g) for g in range(grp)]
        for g in range(grp):
            mix(g, row_max[g])

    pl.when(t < ctx_tiles)(lambda: attend(ctx_len))
    pl.when(t >= ctx_tiles)(lambda: attend(k_ref.shape[0]))


def _attention(cfg, qkv):
    n = qkv.shape[0]
    hd, nh, kvh, t_len = cfg["hd"], cfg["NH"], cfg["KVH"], cfg["T"]
    grp = nh // kvh
    tq = cfg["tq"]
    tiles_per_batch = t_len // tq
    return pl.pallas_call(
        functools.partial(_attn_kernel, grp=grp, hd=hd, ctx_len=cfg["CTX"], ctx_tiles=cfg["CTX"] // tq,
                          kv_chunk=512),
        grid=(cfg["B"], kvh, tiles_per_batch),
        in_specs=[
            pl.BlockSpec((tq, grp * hd), lambda b, kh, t: (b * tiles_per_batch + t, kh)),
            pl.BlockSpec((t_len, hd), lambda b, kh, t: (b, nh + kh)),
            pl.BlockSpec((t_len, hd), lambda b, kh, t: (b, nh + kvh + kh)),
        ],
        out_specs=pl.BlockSpec((tq, grp * hd), lambda b, kh, t: (b * tiles_per_batch + t, kh)),
        out_shape=jax.ShapeDtypeStruct((n, nh * hd), BF16),
        scratch_shapes=[pltpu.VMEM((grp * tq, t_len), F32), pltpu.VMEM((t_len, 2 * hd), BF16)],
        compiler_params=_params(("arbitrary", "arbitrary", "arbitrary")),
        name="attention",
    )(qkv, qkv, qkv)


def _convgate_kernel(b_ref, c_ref, u_ref, cp_ref, up_ref, cn_ref, un_ref, w_ref, o_ref, *,
                     tiles_per_batch, ctx_tiles):
    i = pl.program_id(0)
    tm = b_ref.shape[0]
    tpos = i % tiles_per_batch
    has_prev = jnp.logical_and(tpos != 0, tpos != ctx_tiles)
    has_next = jnp.logical_and(tpos != ctx_tiles - 1, tpos != tiles_per_batch - 1)
    v = c_ref[...] * u_ref[...]
    v_before = jnp.where(has_prev, cp_ref[SUBLANES - 1:SUBLANES, :] * up_ref[SUBLANES - 1:SUBLANES, :], 0.0)
    v_after = jnp.where(has_next, cn_ref[0:1, :] * un_ref[0:1, :], 0.0)
    row = lax.broadcasted_iota(jnp.int32, v.shape, 0)
    v_prev = jnp.where(row == 0, v_before, pltpu.roll(v, 1, axis=0))
    v_next = jnp.where(row == tm - 1, v_after, pltpu.roll(v, tm - 1, axis=0))
    y = v_prev * w_ref[0:1, :] + v * w_ref[1:2, :] + v_next * w_ref[2:3, :]
    o_ref[...] = (b_ref[...] * y).astype(o_ref.dtype)


def _convgate(cfg, z, conv_w):
    n, d3 = z.shape
    d = d3 // 3
    tm = cfg["tm_rows"]
    sub_per_tile = tm // SUBLANES
    last_sub = n // SUBLANES - 1
    tiles_per_batch = cfg["T"] // tm

    def prev_map(col):
        return lambda i: (jnp.maximum(i * sub_per_tile - 1, 0), col)

    def next_map(col):
        return lambda i: (jnp.minimum((i + 1) * sub_per_tile, last_sub), col)

    return pl.pallas_call(
        functools.partial(_convgate_kernel, tiles_per_batch=tiles_per_batch, ctx_tiles=cfg["CTX"] // tm),
        grid=(n // tm,),
        in_specs=[
            pl.BlockSpec((tm, d), lambda i: (i, 0)),
            pl.BlockSpec((tm, d), lambda i: (i, 1)),
            pl.BlockSpec((tm, d), lambda i: (i, 2)),
            pl.BlockSpec((SUBLANES, d), prev_map(1)),
            pl.BlockSpec((SUBLANES, d), prev_map(2)),
            pl.BlockSpec((SUBLANES, d), next_map(1)),
            pl.BlockSpec((SUBLANES, d), next_map(2)),
            pl.BlockSpec((conv_w.shape[0], d), lambda i: (0, 0)),
        ],
        out_specs=pl.BlockSpec((tm, d), lambda i: (i, 0)),
        out_shape=jax.ShapeDtypeStruct((n, d), BF16),
        compiler_params=_params(("arbitrary",)),
        name="convgate",
    )(z, z, z, z, z, z, z, conv_w)


def _route_kernel(lg_ref, meta_ref, wts_ref, cnt_ref, carry_ref, *, n_experts):
    i = pl.program_id(0)

    @pl.when(i == 0)
    def _():
        carry_ref[...] = jnp.zeros_like(carry_ref)

    tm = lg_ref.shape[0]
    logits = lg_ref[...]
    lane = lax.broadcasted_iota(jnp.int32, logits.shape, 1)
    neg = jnp.float32(-jnp.inf)
    lg = jnp.where(lane < n_experts, logits, neg)
    v1 = jnp.max(lg, axis=-1, keepdims=True)
    i1 = jnp.min(jnp.where(lg == v1, lane, LANES), axis=-1, keepdims=True)
    lg2 = jnp.where(lane == i1, neg, lg)
    v2 = jnp.max(lg2, axis=-1, keepdims=True)
    i2 = jnp.min(jnp.where(lg2 == v2, lane, LANES), axis=-1, keepdims=True)
    e = jnp.exp(v2 - v1)
    w1 = 1.0 / (1.0 + e)
    w2 = e / (1.0 + e)
    sel1 = lane == i1
    sel2 = lane == i2
    onehot = jnp.where(jnp.logical_or(sel1, sel2), 1.0, 0.0)
    rr = lax.broadcasted_iota(jnp.int32, (tm, tm), 0)
    cc = lax.broadcasted_iota(jnp.int32, (tm, tm), 1)
    tri = jnp.where(rr > cc, 1.0, 0.0).astype(BF16)
    before = jnp.dot(tri, onehot.astype(BF16), preferred_element_type=F32) + carry_ref[...]
    r1 = jnp.sum(jnp.where(sel1, before, 0.0), axis=-1, keepdims=True).astype(jnp.int32)
    r2 = jnp.sum(jnp.where(sel2, before, 0.0), axis=-1, keepdims=True).astype(jnp.int32)
    carry_ref[...] = carry_ref[...] + jnp.sum(onehot, axis=0, keepdims=True)
    meta_ref[...] = jnp.where(lane == 0, i1, jnp.where(lane == 1, i2, jnp.where(lane == 2, r1,
                              jnp.where(lane == 3, r2, 0))))
    wts_ref[...] = jnp.where(lane == 0, w1, jnp.where(lane == 1, w2, 0.0))
    cnt_ref[...] = jnp.broadcast_to(carry_ref[...], cnt_ref.shape)


def _route(cfg, logits, n_experts):
    n = logits.shape[0]
    tm = cfg["tm_rows"]
    return pl.pallas_call(
        functools.partial(_route_kernel, n_experts=n_experts),
        grid=(n // tm,),
        in_specs=[pl.BlockSpec((tm, LANES), lambda i: (i, 0))],
        out_specs=[pl.BlockSpec((tm, LANES), lambda i: (i, 0)), pl.BlockSpec((tm, LANES), lambda i: (i, 0)),
                   pl.BlockSpec((SUBLANES, LANES), lambda i: (0, 0))],
        out_shape=[jax.ShapeDtypeStruct((n, LANES), jnp.int32), jax.ShapeDtypeStruct((n, LANES), F32),
                   jax.ShapeDtypeStruct((SUBLANES, LANES), F32)],
        scratch_shapes=[pltpu.VMEM((1, LANES), F32)],
        compiler_params=_params(("arbitrary",)),
        name="route",
    )(logits)


GATHER_PITCH_PAD = SUBLANES


def _slab_gather_kernel(idx0_ref, idxn_ref, src_ref, *rest, rows, per_row, nsub, steps, weighted):
    if weighted:
        w_ref, o_ref, buf, sems = rest
    else:
        o_ref, buf, sems = rest
    i = pl.program_id(0)
    slot = i % 2
    pitch = nsub + GATHER_PITCH_PAD
    n_slabs = per_row * rows
    unroll = 8

    def slab_copy(src_slab, dst_slab, sl):
        src0 = pl.multiple_of(src_slab * nsub, nsub)
        dst0 = pl.multiple_of(dst_slab * pitch, SUBLANES)
        return pltpu.make_async_copy(src_ref.at[pl.ds(src0, nsub)], buf.at[sl, pl.ds(dst0, nsub)], sems.at[sl])

    def issue(idx_ref, sl):
        def body(o, carry):
            for u in range(unroll):
                r = o * unroll + u
                slab_copy(idx_ref[0, r], r, sl).start()
            return carry

        lax.fori_loop(0, n_slabs // unroll, body, 0)

    def drain(sl):
        def body(o, carry):
            for u in range(unroll):
                slab_copy(0, 0, sl).wait()
            return carry

        lax.fori_loop(0, n_slabs // unroll, body, 0)

    @pl.when(i == 0)
    def _():
        issue(idx0_ref, 0)

    @pl.when(i + 1 < steps)
    def _():
        issue(idxn_ref, 1 - slot)

    drain(slot)

    if weighted:
        wk = [jnp.broadcast_to(w_ref[:, k:k + 1], (rows, LANES)) for k in range(per_row)]

    for sl in range(2):
        @pl.when(slot == sl)
        def _():
            for s in range(nsub):
                parts = [buf[sl, pl.ds(k * rows * pitch + s, rows, stride=pitch), :] for k in range(per_row)]
                if weighted:
                    v = wk[0] * parts[0]
                    for k in range(1, per_row):
                        v = v + wk[k] * parts[k]
                else:
                    (v,) = parts
                o_ref[:, s * LANES:(s + 1) * LANES] = v.astype(o_ref.dtype)


def _slab_gather(src_slabs, idx, *, d, rows, per_row=1, wts=None, out_dtype=F32, name="slab_gather"):
    n_out = idx.shape[1]
    nsub = d // LANES
    steps = n_out // rows
    pitch = nsub + GATHER_PITCH_PAD
    idx3 = idx.reshape(per_row, steps, rows).transpose(1, 0, 2).reshape(steps, 1, per_row * rows)
    smem_idx = lambda imap: pl.BlockSpec((None, 1, per_row * rows), imap, memory_space=pltpu.SMEM)
    in_specs = [smem_idx(lambda i: (0, 0, 0)),
                smem_idx(lambda i: (jnp.minimum(i + 1, steps - 1), 0, 0)),
                pl.BlockSpec(memory_space=pl.ANY)]
    args = [idx3, idx3, src_slabs]
    if wts is not None:
        in_specs.append(pl.BlockSpec((rows, LANES), lambda i: (i, 0)))
        args.append(wts)
    return pl.pallas_call(
        functools.partial(_slab_gather_kernel, rows=rows, per_row=per_row, nsub=nsub, steps=steps,
                          weighted=wts is not None),
        grid=(steps,),
        in_specs=in_specs,
        out_specs=pl.BlockSpec((rows, d), lambda i: (i, 0)),
        out_shape=jax.ShapeDtypeStruct((n_out, d), out_dtype),
        scratch_shapes=[pltpu.VMEM((2, per_row * rows * pitch, LANES), F32), pltpu.SemaphoreType.DMA((2,))],
        compiler_params=_params(("arbitrary",)),
        name=name,
    )(*args)


def _moe(cfg, logits, h_slabs, n_exp, w_gu, w_down):
    n = logits.shape[0]
    d = w_gu.shape[1]
    tm = cfg["tm_moe"]
    meta, wts, cnt = _route(cfg, logits, n_exp)
    counts = cnt[0, :n_exp].astype(jnp.int32)
    tiles_e = (counts + tm - 1) // tm
    tiles_cum = jnp.cumsum(tiles_e)
    row_off = (tiles_cum - tiles_e) * tm
    pos = jnp.stack([row_off[meta[:, 0]] + meta[:, 2], row_off[meta[:, 1]] + meta[:, 3]])
    n_tiles = (2 * n) // tm + n_exp
    tok = jnp.arange(n, dtype=jnp.int32)
    src = jnp.zeros((n_tiles * tm,), jnp.int32).at[pos.reshape(-1)].set(jnp.concatenate([tok, tok]))
    n_active = tiles_cum[-1:]
    tile_ids = jnp.minimum(jnp.arange(n_tiles, dtype=jnp.int32), n_active[0] - 1)
    tile_expert = jnp.minimum(jnp.sum(tiles_cum[None, :] <= tile_ids[:, None], axis=1), n_exp - 1).astype(jnp.int32)

    xs = _slab_gather(h_slabs, src[None], d=d, rows=cfg["rows_dispatch"], out_dtype=BF16, name="moe_dispatch")
    mid = _grouped_matmul(xs, w_gu, tile_expert, n_active, tm=tm, tn=cfg["tn_ff"], swiglu=True,
                          out_dtype=BF16, name="moe_up")
    out_slabs = _grouped_matmul(mid, w_down, tile_expert, n_active, tm=tm, tn=d, slab=True, name="moe_down")
    return _slab_gather(out_slabs, pos, d=d, rows=cfg["rows_combine"], per_row=2, wts=wts, name="moe_combine")


def _rope_tables(cfg):
    hd, seq, ctx = cfg["hd"], cfg["SEQ"], cfg["CTX"]
    rows = seq // GRID_W
    r, col = jnp.meshgrid(jnp.arange(rows), jnp.arange(GRID_W), indexing="ij")
    r = r.reshape(-1).astype(F32)
    col = col.reshape(-1).astype(F32)
    n_freq = hd // 4
    inv = ROPE_THETA ** (-jnp.arange(n_freq, dtype=F32) / n_freq)
    ang = jnp.concatenate([r[:, None] * inv, col[:, None] * inv], axis=-1)
    cos = jnp.repeat(jnp.cos(ang), 2, axis=-1)
    sin = jnp.repeat(jnp.sin(ang), 2, axis=-1) * jnp.tile(jnp.array([-1.0, 1.0], F32), hd // 2)
    cos = jnp.concatenate([jnp.ones((ctx, hd), F32), cos], axis=0)
    sin = jnp.concatenate([jnp.zeros((ctx, hd), F32), sin], axis=0)
    return cos, sin


def kernel(x, c, ctx, c_ctx, ada_w, ada_b, norm_g, attn_w_qkv, attn_q_norm, attn_k_norm, attn_w_o,
           conv_w_in, conv_w, conv_w_out, ffn_w_gu, ffn_w_down, moe_router, moe_w_gu, moe_w_down):
    nb, seq, d = x.shape
    ctx_len = ctx.shape[1]
    depth = ada_w.shape[0]
    hd = attn_q_norm.shape[1]
    nh = attn_w_o.shape[1] // hd
    kvh = (attn_w_qkv.shape[2] // hd - nh) // 2
    d_ff = ffn_w_down.shape[1]
    t_len = ctx_len + seq
    tm_rows = _pick(ctx_len, (256, 128))

    def stream_cfg(ctx_rows):
        rows = nb * (ctx_rows + seq)
        return dict(B=nb, SEQ=seq, CTX=ctx_rows, T=ctx_rows + seq, hd=hd, NH=nh, KVH=kvh,
                    R=-(-(nb + 1) // SUBLANES) * SUBLANES, tm_rows=tm_rows, tq=_pick(ctx_len, (256, 128)),
                    tm_mm=_pick(rows, (512, 256, 128)), tm_moe=_pick(2 * rows, (512, 256, 128)),
                    tn_ff=_pick(d_ff, (1408, 512, 256, 128)), tn_down=_pick(d, (512, 256, 128)),
                    rows_dispatch=256, rows_combine=128)

    cfg = stream_cfg(ctx_len)
    ctx_free_from = depth - 1 if (depth - 1) % 2 == 1 else depth
    n = nb * t_len

    xa = jnp.concatenate([ctx, x], axis=1).reshape(n, d)
    cvec = jnp.zeros((cfg["R"], d), F32).at[:nb].set(c).at[nb].set(c_ctx)
    mod = _modulation(cvec, ada_w, ada_b)
    mod = mod.reshape(depth, cfg["R"], 6, d).transpose(0, 2, 1, 3).reshape(depth * 6 * cfg["R"], 1, d)
    cos_t, sin_t = _rope_tables(cfg)

    w_qkv = attn_w_qkv.astype(BF16)
    w_o = attn_w_o.astype(BF16)
    w_cin = conv_w_in.astype(BF16)
    w_cout = conv_w_out.astype(BF16)
    w_gu = ffn_w_gu.astype(BF16)
    w_dn = ffn_w_down.astype(BF16)
    m_gu = moe_w_gu.astype(BF16)
    m_dn = moe_w_down.astype(BF16)

    (h,) = _resnorm(cfg, xa, mod, g_pre=norm_g[0, 0], shift_idx=(0, 0), scale_idx=(0, 1))
    for i in range(depth):
        j = i // 2
        moe_layer = i % 2 == 1
        tm = cfg["tm_mm"]
        if i % 2 == 0:
            q_scale = (hd ** -0.5) * 1.4426950408889634
            gain = jnp.concatenate([jnp.tile(attn_q_norm[j] * q_scale, nh), jnp.tile(attn_k_norm[j], kvh),
                                    jnp.ones((kvh * hd,), F32)]).reshape(1, -1)
            qkv = _qkv_proj(cfg, h, w_qkv[j], gain, cos_t, sin_t)
            o = _attention(cfg, qkv)
            y = _dense_matmul(o, w_o[j], tm=tm, tn=cfg["tn_down"], name="attn_out")
        else:
            z = _dense_matmul(h, w_cin[j], tm=tm, tn=_pick(3 * d, (1024, 512, 256, 128)), name="conv_in")
            gated = _convgate(cfg, z, conv_w[j])
            y = _dense_matmul(gated, w_cout[j], tm=tm, tn=cfg["tn_down"], name="conv_out")
        n_exp = moe_router.shape[2]
        router = jnp.zeros((d, LANES), F32).at[:, :n_exp].set(moe_router[j]) if moe_layer else None
        res = _resnorm(cfg, xa, mod, y=y, g_post=norm_g[i, 1], gate_idx=(i, 2), g_pre=norm_g[i, 2],
                       shift_idx=(i, 3), scale_idx=(i, 4), router=router)
        xa, h = res[0], res[1]
        last = i == depth - 1
        nxt = {} if last else dict(g_pre=norm_g[i + 1, 0], shift_idx=(i + 1, 0), scale_idx=(i + 1, 1))
        if moe_layer:
            y = _moe(cfg, res[1], res[2], n_exp, m_gu[j], m_dn[j])
        else:
            mid = _dense_matmul(h, w_gu[j], tm=tm, tn=cfg["tn_ff"], swiglu=True, out_dtype=BF16, name="ffn_up")
            y = _dense_matmul(mid, w_dn[j], tm=tm, tn=cfg["tn_down"], name="ffn_down")
        drop_ctx = cfg["CTX"] > 0 and (last or i + 1 >= ctx_free_from)
        res = _resnorm(cfg, xa, mod, y=y, g_post=norm_g[i, 3], gate_idx=(i, 5), latent_only=drop_ctx, **nxt)
        if drop_ctx:
            cfg = stream_cfg(0)
        xa = res[0]
        if not last:
            h = res[1]
    return xa.reshape(nb, seq, d)
```

```python
import functools

import jax
import jax.numpy as jnp
from jax import lax
from jax.experimental import pallas as pl
from jax.experimental.pallas import tpu as pltpu

EPS = 1e-6
GRID_W = 64
ROPE_THETA = 10000.0
LANES = 128
SUBLANES = 8
VMEM_LIMIT = 56 * 1024 * 1024

F32 = jnp.float32
BF16 = jnp.bfloat16


def _params(sem, vmem=VMEM_LIMIT):
    return pltpu.CompilerParams(dimension_semantics=sem, vmem_limit_bytes=vmem)


def _pick(n, candidates):
    for c in candidates:
        if n % c == 0:
            return c
    raise ValueError(f"no tile in {candidates} divides {n}")


def _mod_kernel(c_ref, w_ref, b_ref, o_ref):
    c = c_ref[...]
    s = c * (1.0 / (1.0 + jnp.exp(-c)))
    o_ref[...] = jnp.dot(s.astype(BF16), w_ref[...].astype(BF16), preferred_element_type=F32) + b_ref[...]


def _modulation(cvec, ada_w, ada_b):
    depth, d, n = ada_w.shape
    r = cvec.shape[0]
    tn = _pick(n, (1024, 512, 256, 128))
    return pl.pallas_call(
        _mod_kernel,
        grid=(depth, n // tn),
        in_specs=[
            pl.BlockSpec((r, d), lambda l, j: (0, 0)),
            pl.BlockSpec((None, d, tn), lambda l, j: (l, 0, j)),
            pl.BlockSpec((None, 1, tn), lambda l, j: (l, 0, j)),
        ],
        out_specs=pl.BlockSpec((None, r, tn), lambda l, j: (l, 0, j)),
        out_shape=jax.ShapeDtypeStruct((depth, r, n), F32),
        compiler_params=_params(("arbitrary", "arbitrary")),
        name="ada_mod",
    )(cvec, ada_w, ada_b.reshape(depth, 1, n))


def _rms(x, g):
    return x * lax.rsqrt(jnp.mean(x * x, axis=-1, keepdims=True) + EPS) * g


def _resnorm_kernel(*refs, n_y, has_h, slab):
    it = iter(refs)
    x = next(it)[...]
    if n_y:
        y = next(it)[...]
        g_post = next(it)[...]
        gate = next(it)[...]
        x = x + gate * _rms(y, g_post)
    if has_h:
        g_pre = next(it)[...]
        shift = next(it)[...]
        scale = next(it)[...]
        if slab:
            router_ref = next(it)
    if n_y:
        next(it)[...] = x
    if has_h:
        h = _rms(x, g_pre) * (1.0 + scale) + shift
        if slab:
            next(it)[...] = jnp.dot(h, router_ref[...], preferred_element_type=F32, precision=lax.Precision.HIGHEST)
            _store_slabs(next(it), h)
        else:
            h_ref = next(it)
            h_ref[...] = h.astype(h_ref.dtype)


def _store_slabs(slab_ref, v):
    rows, d = v.shape
    nsub = d // LANES
    for s in range(nsub):
        slab_ref[pl.ds(s, rows, stride=nsub), :] = v[:, s * LANES:(s + 1) * LANES]


def _resnorm(cfg, x, mod, *, y=None, g_post=None, gate_idx=None, g_pre=None,
             shift_idx=None, scale_idx=None, router=None, latent_only=False):
    d = x.shape[1]
    tm = cfg["tm_rows"]
    tiles_per_batch = cfg["T"] // tm
    ctx_tiles = cfg["CTX"] // tm
    nb, r = cfg["B"], cfg["R"]
    n_y = 0 if y is None else 1
    has_h = g_pre is not None
    slab = router is not None
    out_tiles_per_batch = tiles_per_batch - ctx_tiles if latent_only else tiles_per_batch
    n = nb * out_tiles_per_batch * tm

    def in_tile(i):
        if latent_only:
            return (i // out_tiles_per_batch) * tiles_per_batch + ctx_tiles + i % out_tiles_per_batch
        return i

    def mod_spec(idx):
        l, k = idx

        def imap(i):
            it = in_tile(i)
            row = jnp.where(it % tiles_per_batch < ctx_tiles, nb, it // tiles_per_batch)
            return ((l * 6 + k) * r + row, 0, 0)

        return pl.BlockSpec((None, 1, d), imap)

    row_spec = pl.BlockSpec((tm, d), lambda i: (i, 0))
    in_row_spec = pl.BlockSpec((tm, d), lambda i: (in_tile(i), 0))
    vec_spec = pl.BlockSpec((1, d), lambda i: (0, 0))
    args, specs = [x], [in_row_spec]
    if n_y:
        args += [y, g_post.reshape(1, d), mod]
        specs += [in_row_spec, vec_spec, mod_spec(gate_idx)]
    if has_h:
        args += [g_pre.reshape(1, d), mod, mod]
        specs += [vec_spec, mod_spec(shift_idx), mod_spec(scale_idx)]
        if slab:
            args.append(router)
            specs.append(pl.BlockSpec((d, LANES), lambda i: (0, 0)))
    out_shape, out_specs = [], []
    if n_y:
        out_shape.append(jax.ShapeDtypeStruct((n, d), F32))
        out_specs.append(row_spec)
    if has_h and slab:
        nsub = d // LANES
        out_shape += [jax.ShapeDtypeStruct((n, LANES), F32), jax.ShapeDtypeStruct((n * nsub, LANES), F32)]
        out_specs += [pl.BlockSpec((tm, LANES), lambda i: (i, 0)), pl.BlockSpec((tm * nsub, LANES), lambda i: (i, 0))]
    elif has_h:
        out_shape.append(jax.ShapeDtypeStruct((n, d), BF16))
        out_specs.append(row_spec)
    return pl.pallas_call(
        functools.partial(_resnorm_kernel, n_y=n_y, has_h=has_h, slab=slab),
        grid=(n // tm,),
        in_specs=specs,
        out_specs=out_specs,
        out_shape=out_shape,
        compiler_params=_params(("arbitrary",)),
        name="resnorm",
    )(*args)


def _mm_plain_kernel(te_ref, na_ref, a_ref, w_ref, o_ref, *, slab):
    i = pl.program_id(1)

    @pl.when(i < na_ref[0])
    def _():
        acc = jnp.dot(a_ref[...], w_ref[...], preferred_element_type=F32)
        if slab:
            _store_slabs(o_ref, acc)
        else:
            o_ref[...] = acc.astype(o_ref.dtype)

    @pl.when(i >= na_ref[0])
    def _():
        o_ref[...] = jnp.zeros_like(o_ref)


CAST_ROWS = 256


def _mm_swiglu_kernel(te_ref, na_ref, a_ref, wg_ref, wu_ref, o_ref, wg_b, wu_b):
    i = pl.program_id(1)
    new_block = jnp.logical_or(i == 0, te_ref[i] != te_ref[jnp.maximum(i - 1, 0)])

    @pl.when(new_block)
    def _():
        for r0 in range(0, wg_ref.shape[0], CAST_ROWS):
            wg_b[r0:r0 + CAST_ROWS, :] = wg_ref[r0:r0 + CAST_ROWS, :].astype(BF16)
            wu_b[r0:r0 + CAST_ROWS, :] = wu_ref[r0:r0 + CAST_ROWS, :].astype(BF16)

    @pl.when(i < na_ref[0])
    def _():
        a = a_ref[...]
        g = jnp.dot(a, wg_b[...], preferred_element_type=F32)
        u = jnp.dot(a, wu_b[...], preferred_element_type=F32)
        o_ref[...] = (g * (1.0 / (1.0 + jnp.exp(-g))) * u).astype(o_ref.dtype)

    @pl.when(i >= na_ref[0])
    def _():
        o_ref[...] = jnp.zeros_like(o_ref)


def _grouped_matmul(a, w, tile_expert, n_active, *, tm, tn, swiglu=False, slab=False, out_dtype=F32, name="mm"):
    m, k = a.shape
    n_out = w.shape[2] // 2 if swiglu else w.shape[2]
    nj, ni = n_out // tn, m // tm

    def a_map(j, i, te, na):
        return (jnp.minimum(i, na[0] - 1), 0)

    def w_map(off):
        return lambda j, i, te, na: (te[i], 0, j + off)

    w_mode = dict(pipeline_mode=pl.Buffered(1)) if (slab or swiglu) else {}
    w_spec = lambda off: pl.BlockSpec((None, k, tn), w_map(off), **w_mode)
    in_specs = [pl.BlockSpec((tm, k), a_map), w_spec(0)]
    args = [a, w]
    scratch = []
    if swiglu:
        in_specs.append(w_spec(nj))
        args.append(w)
        body = _mm_swiglu_kernel
        scratch = [pltpu.VMEM((k, tn), BF16), pltpu.VMEM((k, tn), BF16)]
    else:
        body = functools.partial(_mm_plain_kernel, slab=slab)
    if slab:
        assert nj == 1 and out_dtype == F32
        nsub = n_out // LANES
        out_shape = jax.ShapeDtypeStruct((m * nsub, LANES), F32)
        out_spec = pl.BlockSpec((tm * nsub, LANES), lambda j, i, te, na: (i, 0))
    else:
        out_shape = jax.ShapeDtypeStruct((m, n_out), out_dtype)
        out_spec = pl.BlockSpec((tm, tn), lambda j, i, te, na: (i, j))
    return pl.pallas_call(
        body,
        grid_spec=pltpu.PrefetchScalarGridSpec(
            num_scalar_prefetch=2,
            grid=(nj, ni),
            in_specs=in_specs,
            out_specs=out_spec,
            scratch_shapes=scratch,
        ),
        out_shape=out_shape,
        compiler_params=_params(("arbitrary", "arbitrary")),
        name=name,
    )(tile_expert, n_active, *args)


def _dense_matmul(a, w, *, tm, tn, group=0, swiglu=False, out_dtype=F32, name="mm"):
    ni = a.shape[0] // tm
    if w.ndim == 2:
        w = w[None]
    return _grouped_matmul(a, w, jnp.full((ni,), group, jnp.int32), jnp.full((1,), ni, jnp.int32),
                           tm=tm, tn=tn, swiglu=swiglu, out_dtype=out_dtype, name=name)


def _qkv_kernel(a_ref, w_ref, g_ref, cos_ref, sin_ref, o_ref, *, n_norm_tiles, hd, sub):
    j = pl.program_id(0)
    tm = a_ref.shape[0]

    def run(norm):
        for r0 in range(0, tm, sub):
            acc = jnp.dot(a_ref[r0:r0 + sub, :], w_ref[...], preferred_element_type=F32)
            if not norm:
                o_ref[r0:r0 + sub, :] = acc.astype(o_ref.dtype)
                continue
            cos = cos_ref[r0:r0 + sub, :]
            sin = sin_ref[r0:r0 + sub, :]
            even = (lax.broadcasted_iota(jnp.int32, cos.shape, 1) % 2) == 0
            for c in range(acc.shape[1] // hd):
                x = acc[:, c * hd:(c + 1) * hd]
                x = x * lax.rsqrt(jnp.mean(x * x, axis=-1, keepdims=True) + EPS) * g_ref[:, c * hd:(c + 1) * hd]
                partner = jnp.where(even, pltpu.roll(x, hd - 1, axis=1), pltpu.roll(x, 1, axis=1))
                o_ref[r0:r0 + sub, c * hd:(c + 1) * hd] = (x * cos + partner * sin).astype(o_ref.dtype)

    pl.when(j < n_norm_tiles)(lambda: run(True))
    pl.when(j >= n_norm_tiles)(lambda: run(False))


def _qkv_proj(cfg, h, w_qkv, gain, cos_t, sin_t):
    n, d = h.shape
    nq = w_qkv.shape[1]
    hd, nh, kvh = cfg["hd"], cfg["NH"], cfg["KVH"]
    tm = cfg["tm_rows"]
    tn = _pick(kvh * hd, (512, 256, 128))
    tiles_per_batch = cfg["T"] // tm
    return pl.pallas_call(
        functools.partial(_qkv_kernel, n_norm_tiles=(nh + kvh) * hd // tn, hd=hd, sub=min(tm, 128)),
        grid=(nq // tn, n // tm),
        in_specs=[
            pl.BlockSpec((tm, d), lambda j, i: (i, 0)),
            pl.BlockSpec((d, tn), lambda j, i: (0, j)),
            pl.BlockSpec((1, tn), lambda j, i: (0, j)),
            pl.BlockSpec((tm, hd), lambda j, i: (i % tiles_per_batch, 0)),
            pl.BlockSpec((tm, hd), lambda j, i: (i % tiles_per_batch, 0)),
        ],
        out_specs=pl.BlockSpec((tm, tn), lambda j, i: (i, j)),
        out_shape=jax.ShapeDtypeStruct((n, nq), BF16),
        compiler_params=_params(("arbitrary", "arbitrary")),
        name="qkv_proj",
    )(h, w_qkv, gain, cos_t, sin_t)


def _attn_kernel(q_ref, k_ref, v_ref, o_ref, s_ref, vx_ref, *, grp, hd, ctx_len, ctx_tiles, kv_chunk):
    t = pl.program_id(2)
    tq = q_ref.shape[0]

    @pl.when(t == 0)
    def _():
        lane = lax.broadcasted_iota(jnp.int32, (v_ref.shape[0], hd), 1)
        vx_ref[:, :hd] = v_ref[...]
        vx_ref[:, hd:] = jnp.where(lane == 0, 1.0, 0.0).astype(vx_ref.dtype)

    def attend(kv_len):
        chunks = [(s, min(kv_chunk, kv_len - s)) for s in range(0, kv_len, kv_chunk)]

        def scores(g):
            q = q_ref[:, g * hd:(g + 1) * hd]
            m = None
            for s0, sz in chunks:
                s = lax.dot_general(q, k_ref[s0:s0 + sz, :], (((1,), (1,)), ((), ())),
                                    preferred_element_type=F32)
                s_ref[g * tq:(g + 1) * tq, s0:s0 + sz] = s
                for c in range(sz // LANES):
                    piece = s[:, c * LANES:(c + 1) * LANES]
                    m = piece if m is None else jnp.maximum(m, piece)
            return jnp.max(m, axis=-1, keepdims=True)

        def mix(g, m):
            acc = jnp.zeros((tq, 2 * hd), F32)
            for s0, sz in chunks:
                p = jnp.exp2(s_ref[g * tq:(g + 1) * tq, s0:s0 + sz] - m)
                acc = acc + jnp.dot(p.astype(BF16), vx_ref[s0:s0 + sz, :], preferred_element_type=F32)
            o = acc[:, :hd] * (1.0 / acc[:, hd:hd + 1])
            o_ref[:, g * hd:(g + 1) * hd] = o.astype(o_ref.dtype)

        row_max = [scores(g) for g in range(grp)]
        for g in range(grp):
            mix(g, row_max[g])

    pl.when(t < ctx_tiles)(lambda: attend(ctx_len))
    pl.when(t >= ctx_tiles)(lambda: attend(k_ref.shape[0]))


def _attention(cfg, qkv):
    n = qkv.shape[0]
    hd, nh, kvh, t_len = cfg["hd"], cfg["NH"], cfg["KVH"], cfg["T"]
    grp = nh // kvh
    tq = cfg["tq"]
    tiles_per_batch = t_len // tq
    return pl.pallas_call(
        functools.partial(_attn_kernel, grp=grp, hd=hd, ctx_len=cfg["CTX"], ctx_tiles=cfg["CTX"] // tq,
                          kv_chunk=512),
        grid=(cfg["B"], kvh, tiles_per_batch),
        in_specs=[
            pl.BlockSpec((tq, grp * hd), lambda b, kh, t: (b * tiles_per_batch + t, kh)),
            pl.BlockSpec((t_len, hd), lambda b, kh, t: (b, nh + kh)),
            pl.BlockSpec((t_len, hd), lambda b, kh, t: (b, nh + kvh + kh)),
        ],
        out_specs=pl.BlockSpec((tq, grp * hd), lambda b, kh, t: (b * tiles_per_batch + t, kh)),
        out_shape=jax.ShapeDtypeStruct((n, nh * hd), BF16),
        scratch_shapes=[pltpu.VMEM((grp * tq, t_len), F32), pltpu.VMEM((t_len, 2 * hd), BF16)],
        compiler_params=_params(("arbitrary", "arbitrary", "arbitrary")),
        name="attention",
    )(qkv, qkv, qkv)


def _convgate_kernel(b_ref, c_ref, u_ref, cp_ref, up_ref, cn_ref, un_ref, w_ref, o_ref, *,
                     tiles_per_batch, ctx_tiles):
    i = pl.program_id(0)
    tm = b_ref.shape[0]
    tpos = i % tiles_per_batch
    has_prev = jnp.logical_and(tpos != 0, tpos != ctx_tiles)
    has_next = jnp.logical_and(tpos != ctx_tiles - 1, tpos != tiles_per_batch - 1)
    v = c_ref[...] * u_ref[...]
    v_before = jnp.where(has_prev, cp_ref[SUBLANES - 1:SUBLANES, :] * up_ref[SUBLANES - 1:SUBLANES, :], 0.0)
    v_after = jnp.where(has_next, cn_ref[0:1, :] * un_ref[0:1, :], 0.0)
    row = lax.broadcasted_iota(jnp.int32, v.shape, 0)
    v_prev = jnp.where(row == 0, v_before, pltpu.roll(v, 1, axis=0))
    v_next = jnp.where(row == tm - 1, v_after, pltpu.roll(v, tm - 1, axis=0))
    y = v_prev * w_ref[0:1, :] + v * w_ref[1:2, :] + v_next * w_ref[2:3, :]
    o_ref[...] = (b_ref[...] * y).astype(o_ref.dtype)


def _convgate(cfg, z, conv_w):
    n, d3 = z.shape
    d = d3 // 3
    tm = cfg["tm_rows"]
    sub_per_tile = tm // SUBLANES
    last_sub = n // SUBLANES - 1
    tiles_per_batch = cfg["T"] // tm

    def prev_map(col):
        return lambda i: (jnp.maximum(i * sub_per_tile - 1, 0), col)

    def next_map(col):
        return lambda i: (jnp.minimum((i + 1) * sub_per_tile, last_sub), col)

    return pl.pallas_call(
        functools.partial(_convgate_kernel, tiles_per_batch=tiles_per_batch, ctx_tiles=cfg["CTX"] // tm),
        grid=(n // tm,),
        in_specs=[
            pl.BlockSpec((tm, d), lambda i: (i, 0)),
            pl.BlockSpec((tm, d), lambda i: (i, 1)),
            pl.BlockSpec((tm, d), lambda i: (i, 2)),
            pl.BlockSpec((SUBLANES, d), prev_map(1)),
            pl.BlockSpec((SUBLANES, d), prev_map(2)),
            pl.BlockSpec((SUBLANES, d), next_map(1)),
            pl.BlockSpec((SUBLANES, d), next_map(2)),
            pl.BlockSpec((conv_w.shape[0], d), lambda i: (0, 0)),
        ],
        out_specs=pl.BlockSpec((tm, d), lambda i: (i, 0)),
        out_shape=jax.ShapeDtypeStruct((n, d), BF16),
        compiler_params=_params(("arbitrary",)),
        name="convgate",
    )(z, z, z, z, z, z, z, conv_w)


def _route_kernel(lg_ref, meta_ref, wts_ref, cnt_ref, carry_ref, *, n_experts):
    i = pl.program_id(0)

    @pl.when(i == 0)
    def _():
        carry_ref[...] = jnp.zeros_like(carry_ref)

    tm = lg_ref.shape[0]
    logits = lg_ref[...]
    lane = lax.broadcasted_iota(jnp.int32, logits.shape, 1)
    neg = jnp.float32(-jnp.inf)
    lg = jnp.where(lane < n_experts, logits, neg)
    v1 = jnp.max(lg, axis=-1, keepdims=True)
    i1 = jnp.min(jnp.where(lg == v1, lane, LANES), axis=-1, keepdims=True)
    lg2 = jnp.where(lane == i1, neg, lg)
    v2 = jnp.max(lg2, axis=-1, keepdims=True)
    i2 = jnp.min(jnp.where(lg2 == v2, lane, LANES), axis=-1, keepdims=True)
    e = jnp.exp(v2 - v1)
    w1 = 1.0 / (1.0 + e)
    w2 = e / (1.0 + e)
    sel1 = lane == i1
    sel2 = lane == i2
    onehot = jnp.where(jnp.logical_or(sel1, sel2), 1.0, 0.0)
    rr = lax.broadcasted_iota(jnp.int32, (tm, tm), 0)
    cc = lax.broadcasted_iota(jnp.int32, (tm, tm), 1)
    tri = jnp.where(rr > cc, 1.0, 0.0).astype(BF16)
    before = jnp.dot(tri, onehot.astype(BF16), preferred_element_type=F32) + carry_ref[...]
    r1 = jnp.sum(jnp.where(sel1, before, 0.0), axis=-1, keepdims=True).astype(jnp.int32)
    r2 = jnp.sum(jnp.where(sel2, before, 0.0), axis=-1, keepdims=True).astype(jnp.int32)
    carry_ref[...] = carry_ref[...] + jnp.sum(onehot, axis=0, keepdims=True)
    meta_ref[...] = jnp.where(lane == 0, i1, jnp.where(lane == 1, i2, jnp.where(lane == 2, r1,
                              jnp.where(lane == 3, r2, 0))))
    wts_ref[...] = jnp.where(lane == 0, w1, jnp.where(lane == 1, w2, 0.0))
    cnt_ref[...] = jnp.broadcast_to(carry_ref[...], cnt_ref.shape)


def _route(cfg, logits, n_experts):
    n = logits.shape[0]
    tm = cfg["tm_rows"]
    return pl.pallas_call(
        functools.partial(_route_kernel, n_experts=n_experts),
        grid=(n // tm,),
        in_specs=[pl.BlockSpec((tm, LANES), lambda i: (i, 0))],
        out_specs=[pl.BlockSpec((tm, LANES), lambda i: (i, 0)), pl.BlockSpec((tm, LANES), lambda i: (i, 0)),
                   pl.BlockSpec((SUBLANES, LANES), lambda i: (0, 0))],
        out_shape=[jax.ShapeDtypeStruct((n, LANES), jnp.int32), jax.ShapeDtypeStruct((n, LANES), F32),
                   jax.ShapeDtypeStruct((SUBLANES, LANES), F32)],
        scratch_shapes=[pltpu.VMEM((1, LANES), F32)],
        compiler_params=_params(("arbitrary",)),
        name="route",
    )(logits)


GATHER_PITCH_PAD = SUBLANES


def _slab_gather_kernel(idx0_ref, idxn_ref, src_ref, *rest, rows, per_row, nsub, steps, weighted):
    if weighted:
        w_ref, o_ref, buf, sems = rest
    else:
        o_ref, buf, sems = rest
    i = pl.program_id(0)
    slot = i % 2
    pitch = nsub + GATHER_PITCH_PAD
    n_slabs = per_row * rows
    unroll = 8

    def slab_copy(src_slab, dst_slab, sl):
        src0 = pl.multiple_of(src_slab * nsub, nsub)
        dst0 = pl.multiple_of(dst_slab * pitch, SUBLANES)
        return pltpu.make_async_copy(src_ref.at[pl.ds(src0, nsub)], buf.at[sl, pl.ds(dst0, nsub)], sems.at[sl])

    def issue(idx_ref, sl):
        def body(o, carry):
            for u in range(unroll):
                r = o * unroll + u
                slab_copy(idx_ref[0, r], r, sl).start(priority=u % 2)
            return carry

        lax.fori_loop(0, n_slabs // unroll, body, 0)

    def drain(sl):
        def body(o, carry):
            for u in range(unroll):
                slab_copy(0, 0, sl).wait()
            return carry

        lax.fori_loop(0, n_slabs // unroll, body, 0)

    @pl.when(i == 0)
    def _():
        issue(idx0_ref, 0)

    @pl.when(i + 1 < steps)
    def _():
        issue(idxn_ref, 1 - slot)

    drain(slot)

    if weighted:
        wk = [jnp.broadcast_to(w_ref[:, k:k + 1], (rows, LANES)) for k in range(per_row)]

    for sl in range(2):
        @pl.when(slot == sl)
        def _():
            for s in range(nsub):
                parts = [buf[sl, pl.ds(k * rows * pitch + s, rows, stride=pitch), :] for k in range(per_row)]
                if weighted:
                    v = wk[0] * parts[0]
                    for k in range(1, per_row):
                        v = v + wk[k] * parts[k]
                else:
                    (v,) = parts
                o_ref[:, s * LANES:(s + 1) * LANES] = v.astype(o_ref.dtype)


def _slab_gather(src_slabs, idx, *, d, rows, per_row=1, wts=None, out_dtype=F32, name="slab_gather"):
    n_out = idx.shape[1]
    nsub = d // LANES
    steps = n_out // rows
    pitch = nsub + GATHER_PITCH_PAD
    idx3 = idx.reshape(per_row, steps, rows).transpose(1, 0, 2).reshape(steps, 1, per_row * rows)
    smem_idx = lambda imap: pl.BlockSpec((None, 1, per_row * rows), imap, memory_space=pltpu.SMEM)
    in_specs = [smem_idx(lambda i: (0, 0, 0)),
                smem_idx(lambda i: (jnp.minimum(i + 1, steps - 1), 0, 0)),
                pl.BlockSpec(memory_space=pl.ANY)]
    args = [idx3, idx3, src_slabs]
    if wts is not None:
        in_specs.append(pl.BlockSpec((rows, LANES), lambda i: (i, 0)))
        args.append(wts)
    return pl.pallas_call(
        functools.partial(_slab_gather_kernel, rows=rows, per_row=per_row, nsub=nsub, steps=steps,
                          weighted=wts is not None),
        grid=(steps,),
        in_specs=in_specs,
        out_specs=pl.BlockSpec((rows, d), lambda i: (i, 0)),
        out_shape=jax.ShapeDtypeStruct((n_out, d), out_dtype),
        scratch_shapes=[pltpu.VMEM((2, per_row * rows * pitch, LANES), F32), pltpu.SemaphoreType.DMA((2,))],
        compiler_params=_params(("arbitrary",)),
        name=name,
    )(*args)


def _moe(cfg, logits, h_slabs, n_exp, layer, w_gu, w_down):
    n = logits.shape[0]
    d = w_gu.shape[1]
    tm = cfg["tm_moe"]
    meta, wts, cnt = _route(cfg, logits, n_exp)
    counts = cnt[0, :n_exp].astype(jnp.int32)
    tiles_e = (counts + tm - 1) // tm
    tiles_cum = jnp.cumsum(tiles_e)
    row_off = (tiles_cum - tiles_e) * tm
    pos = jnp.stack([row_off[meta[:, 0]] + meta[:, 2], row_off[meta[:, 1]] + meta[:, 3]])
    n_tiles = (2 * n) // tm + n_exp
    tok = jnp.arange(n, dtype=jnp.int32)
    src = jnp.zeros((n_tiles * tm,), jnp.int32).at[pos.reshape(-1)].set(jnp.concatenate([tok, tok]))
    n_active = tiles_cum[-1:]
    tile_ids = jnp.minimum(jnp.arange(n_tiles, dtype=jnp.int32), n_active[0] - 1)
    tile_expert = jnp.minimum(jnp.sum(tiles_cum[None, :] <= tile_ids[:, None], axis=1), n_exp - 1).astype(jnp.int32)
    tile_expert = tile_expert + layer * n_exp

    xs = _slab_gather(h_slabs, src[None], d=d, rows=cfg["rows_dispatch"], out_dtype=BF16, name="moe_dispatch")
    mid = _grouped_matmul(xs, w_gu, tile_expert, n_active, tm=tm, tn=cfg["tn_ff"], swiglu=True,
                          out_dtype=BF16, name="moe_up")
    out_slabs = _grouped_matmul(mid, w_down, tile_expert, n_active, tm=tm, tn=d, slab=True, name="moe_down")
    return _slab_gather(out_slabs, pos, d=d, rows=cfg["rows_combine"], per_row=2, wts=wts, name="moe_combine")


def _rope_tables(cfg):
    hd, seq, ctx = cfg["hd"], cfg["SEQ"], cfg["CTX"]
    rows = seq // GRID_W
    r, col = jnp.meshgrid(jnp.arange(rows), jnp.arange(GRID_W), indexing="ij")
    r = r.reshape(-1).astype(F32)
    col = col.reshape(-1).astype(F32)
    n_freq = hd // 4
    inv = ROPE_THETA ** (-jnp.arange(n_freq, dtype=F32) / n_freq)
    ang = jnp.concatenate([r[:, None] * inv, col[:, None] * inv], axis=-1)
    cos = jnp.repeat(jnp.cos(ang), 2, axis=-1)
    sin = jnp.repeat(jnp.sin(ang), 2, axis=-1) * jnp.tile(jnp.array([-1.0, 1.0], F32), hd // 2)
    cos = jnp.concatenate([jnp.ones((ctx, hd), F32), cos], axis=0)
    sin = jnp.concatenate([jnp.zeros((ctx, hd), F32), sin], axis=0)
    return cos, sin


def kernel(x, c, ctx, c_ctx, ada_w, ada_b, norm_g, attn_w_qkv, attn_q_norm, attn_k_norm, attn_w_o,
           conv_w_in, conv_w, conv_w_out, ffn_w_gu, ffn_w_down, moe_router, moe_w_gu, moe_w_down):
    nb, seq, d = x.shape
    ctx_len = ctx.shape[1]
    depth = ada_w.shape[0]
    hd = attn_q_norm.shape[1]
    nh = attn_w_o.shape[1] // hd
    kvh = (attn_w_qkv.shape[2] // hd - nh) // 2
    d_ff = ffn_w_down.shape[1]
    t_len = ctx_len + seq
    tm_rows = _pick(ctx_len, (256, 128))

    def stream_cfg(ctx_rows):
        rows = nb * (ctx_rows + seq)
        return dict(B=nb, SEQ=seq, CTX=ctx_rows, T=ctx_rows + seq, hd=hd, NH=nh, KVH=kvh,
                    R=-(-(nb + 1) // SUBLANES) * SUBLANES, tm_rows=tm_rows, tq=_pick(ctx_len, (256, 128)),
                    tm_mm=_pick(rows, (512, 256, 128)), tm_moe=_pick(2 * rows, (512, 256, 128)),
                    tn_ff=_pick(d_ff, (1408, 512, 256, 128)), tn_down=_pick(d, (512, 256, 128)),
                    rows_dispatch=256, rows_combine=128)

    cfg = stream_cfg(ctx_len)
    ctx_free_from = depth - 1 if (depth - 1) % 2 == 1 else depth
    n = nb * t_len

    xa = jnp.concatenate([ctx, x], axis=1).reshape(n, d)
    cvec = jnp.zeros((cfg["R"], d), F32).at[:nb].set(c).at[nb].set(c_ctx)
    mod = _modulation(cvec, ada_w, ada_b)
    mod = mod.reshape(depth, cfg["R"], 6, d).transpose(0, 2, 1, 3).reshape(depth * 6 * cfg["R"], 1, d)
    cos_t, sin_t = _rope_tables(cfg)

    w_qkv = attn_w_qkv.astype(BF16)
    w_o = attn_w_o.astype(BF16)
    w_cin = conv_w_in.astype(BF16)
    w_cout = conv_w_out.astype(BF16)
    w_dn = ffn_w_down.astype(BF16)
    n_exp = moe_router.shape[2]
    m_gu = moe_w_gu.reshape((-1,) + moe_w_gu.shape[2:])
    m_dn = moe_w_down.astype(BF16).reshape((-1,) + moe_w_down.shape[2:])

    (h,) = _resnorm(cfg, xa, mod, g_pre=norm_g[0, 0], shift_idx=(0, 0), scale_idx=(0, 1))
    for i in range(depth):
        j = i // 2
        moe_layer = i % 2 == 1
        tm = cfg["tm_mm"]
        if i % 2 == 0:
            q_scale = (hd ** -0.5) * 1.4426950408889634
            gain = jnp.concatenate([jnp.tile(attn_q_norm[j] * q_scale, nh), jnp.tile(attn_k_norm[j], kvh),
                                    jnp.ones((kvh * hd,), F32)]).reshape(1, -1)
            qkv = _qkv_proj(cfg, h, w_qkv[j], gain, cos_t, sin_t)
            o = _attention(cfg, qkv)
            y = _dense_matmul(o, w_o, group=j, tm=tm, tn=cfg["tn_down"], name="attn_out")
        else:
            z = _dense_matmul(h, w_cin, group=j, tm=tm, tn=_pick(3 * d, (1024, 512, 256, 128)), name="conv_in")
            gated = _convgate(cfg, z, conv_w[j])
            y = _dense_matmul(gated, w_cout, group=j, tm=tm, tn=cfg["tn_down"], name="conv_out")
        router = jnp.zeros((d, LANES), F32).at[:, :n_exp].set(moe_router[j]) if moe_layer else None
        res = _resnorm(cfg, xa, mod, y=y, g_post=norm_g[i, 1], gate_idx=(i, 2), g_pre=norm_g[i, 2],
                       shift_idx=(i, 3), scale_idx=(i, 4), router=router)
        xa, h = res[0], res[1]
        last = i == depth - 1
        nxt = {} if last else dict(g_pre=norm_g[i + 1, 0], shift_idx=(i + 1, 0), scale_idx=(i + 1, 1))
        if moe_layer:
            y = _moe(cfg, res[1], res[2], n_exp, j, m_gu, m_dn)
        else:
            mid = _dense_matmul(h, ffn_w_gu, group=j, tm=tm, tn=cfg["tn_ff"], swiglu=True, out_dtype=BF16,
                                name="ffn_up")
            y = _dense_matmul(mid, w_dn, group=j, tm=tm, tn=cfg["tn_down"], name="ffn_down")
        drop_ctx = cfg["CTX"] > 0 and (last or i + 1 >= ctx_free_from)
        res = _resnorm(cfg, xa, mod, y=y, g_post=norm_g[i, 3], gate_idx=(i, 5), latent_only=drop_ctx, **nxt)
        if drop_ctx:
            cfg = stream_cfg(0)
        xa = res[0]
        if not last:
            h = res[1]
    return xa.reshape(nb, seq, d)
```

```python
import functools

import jax
import jax.numpy as jnp
from jax import lax
from jax.experimental import pallas as pl
from jax.experimental.pallas import tpu as pltpu

EPS = 1e-6
GRID_W = 64
ROPE_THETA = 10000.0
LANES = 128
SUBLANES = 8
VMEM_LIMIT = 56 * 1024 * 1024

F32 = jnp.float32
BF16 = jnp.bfloat16


def _params(sem, vmem=VMEM_LIMIT):
    return pltpu.CompilerParams(dimension_semantics=sem, vmem_limit_bytes=vmem)


def _pick(n, candidates):
    for c in candidates:
        if n % c == 0:
            return c
    raise ValueError(f"no tile in {candidates} divides {n}")


def _mod_kernel(c_ref, w_ref, b_ref, o_ref):
    c = c_ref[...]
    s = c * (1.0 / (1.0 + jnp.exp(-c)))
    o_ref[...] = jnp.dot(s.astype(BF16), w_ref[...].astype(BF16), preferred_element_type=F32) + b_ref[...]


def _modulation(cvec, ada_w, ada_b):
    depth, d, n = ada_w.shape
    r = cvec.shape[0]
    tn = _pick(n, (1024, 512, 256, 128))
    return pl.pallas_call(
        _mod_kernel,
        grid=(depth, n // tn),
        in_specs=[
            pl.BlockSpec((r, d), lambda l, j: (0, 0)),
            pl.BlockSpec((None, d, tn), lambda l, j: (l, 0, j)),
            pl.BlockSpec((None, 1, tn), lambda l, j: (l, 0, j)),
        ],
        out_specs=pl.BlockSpec((None, r, tn), lambda l, j: (l, 0, j)),
        out_shape=jax.ShapeDtypeStruct((depth, r, n), F32),
        compiler_params=_params(("arbitrary", "arbitrary")),
        name="ada_mod",
    )(cvec, ada_w, ada_b.reshape(depth, 1, n))


def _rms(x, g):
    return x * lax.rsqrt(jnp.mean(x * x, axis=-1, keepdims=True) + EPS) * g


def _resnorm_kernel(*refs, n_y, has_h, slab):
    it = iter(refs)
    x = next(it)[...]
    if n_y:
        y = next(it)[...]
        g_post = next(it)[...]
        gate = next(it)[...]
        x = x + gate * _rms(y, g_post)
    if has_h:
        g_pre = next(it)[...]
        shift = next(it)[...]
        scale = next(it)[...]
        if slab:
            router_ref = next(it)
    if n_y:
        next(it)[...] = x
    if has_h:
        h = _rms(x, g_pre) * (1.0 + scale) + shift
        if slab:
            next(it)[...] = jnp.dot(h, router_ref[...], preferred_element_type=F32, precision=lax.Precision.HIGHEST)
            _store_slabs(next(it), h)
        else:
            h_ref = next(it)
            h_ref[...] = h.astype(h_ref.dtype)


def _store_slabs(slab_ref, v):
    rows, d = v.shape
    nsub = d // LANES
    for s in range(nsub):
        slab_ref[pl.ds(s, rows, stride=nsub), :] = v[:, s * LANES:(s + 1) * LANES]


def _resnorm(cfg, x, mod, *, y=None, g_post=None, gate_idx=None, g_pre=None,
             shift_idx=None, scale_idx=None, router=None, latent_only=False):
    d = x.shape[1]
    tm = cfg["tm_rows"]
    tiles_per_batch = cfg["T"] // tm
    ctx_tiles = cfg["CTX"] // tm
    nb, r = cfg["B"], cfg["R"]
    n_y = 0 if y is None else 1
    has_h = g_pre is not None
    slab = router is not None
    out_tiles_per_batch = tiles_per_batch - ctx_tiles if latent_only else tiles_per_batch
    n = nb * out_tiles_per_batch * tm

    def in_tile(i):
        if latent_only:
            return (i // out_tiles_per_batch) * tiles_per_batch + ctx_tiles + i % out_tiles_per_batch
        return i

    def mod_spec(idx):
        l, k = idx

        def imap(i):
            it = in_tile(i)
            row = jnp.where(it % tiles_per_batch < ctx_tiles, nb, it // tiles_per_batch)
            return ((l * 6 + k) * r + row, 0, 0)

        return pl.BlockSpec((None, 1, d), imap)

    row_spec = pl.BlockSpec((tm, d), lambda i: (i, 0))
    in_row_spec = pl.BlockSpec((tm, d), lambda i: (in_tile(i), 0))
    vec_spec = pl.BlockSpec((1, d), lambda i: (0, 0))
    args, specs = [x], [in_row_spec]
    if n_y:
        args += [y, g_post.reshape(1, d), mod]
        specs += [in_row_spec, vec_spec, mod_spec(gate_idx)]
    if has_h:
        args += [g_pre.reshape(1, d), mod, mod]
        specs += [vec_spec, mod_spec(shift_idx), mod_spec(scale_idx)]
        if slab:
            args.append(router)
            specs.append(pl.BlockSpec((d, LANES), lambda i: (0, 0)))
    out_shape, out_specs = [], []
    if n_y:
        out_shape.append(jax.ShapeDtypeStruct((n, d), F32))
        out_specs.append(row_spec)
    if has_h and slab:
        nsub = d // LANES
        out_shape += [jax.ShapeDtypeStruct((n, LANES), F32), jax.ShapeDtypeStruct((n * nsub, LANES), F32)]
        out_specs += [pl.BlockSpec((tm, LANES), lambda i: (i, 0)), pl.BlockSpec((tm * nsub, LANES), lambda i: (i, 0))]
    elif has_h:
        out_shape.append(jax.ShapeDtypeStruct((n, d), BF16))
        out_specs.append(row_spec)
    return pl.pallas_call(
        functools.partial(_resnorm_kernel, n_y=n_y, has_h=has_h, slab=slab),
        grid=(n // tm,),
        in_specs=specs,
        out_specs=out_specs,
        out_shape=out_shape,
        compiler_params=_params(("arbitrary",)),
        name="resnorm",
    )(*args)


def _mm_plain_kernel(te_ref, na_ref, a_ref, w_ref, o_ref, *, slab):
    i = pl.program_id(1)

    @pl.when(i < na_ref[0])
    def _():
        acc = jnp.dot(a_ref[...], w_ref[...], preferred_element_type=F32)
        if slab:
            _store_slabs(o_ref, acc)
        else:
            o_ref[...] = acc.astype(o_ref.dtype)

    @pl.when(i >= na_ref[0])
    def _():
        o_ref[...] = jnp.zeros_like(o_ref)


CAST_ROWS = 256


def _mm_swiglu_kernel(te_ref, nx_ref, na_ref, a_ref, w_hbm, o_ref, stage_g, stage_u, wb_g, wb_u, sems, *,
                      tn, nj):
    j = pl.program_id(0)
    i = pl.program_id(1)
    new_block = jnp.logical_or(i == 0, te_ref[i] != te_ref[jnp.maximum(i - 1, 0)])
    stage = (stage_g, stage_u)
    wb = (wb_g, wb_u)

    def fetch(expert, col_tile):
        copies = []
        for half in range(2):
            col0 = pl.multiple_of((col_tile + half * nj) * tn, LANES)
            copies.append(pltpu.make_async_copy(w_hbm.at[expert, :, pl.ds(col0, tn)], stage[half],
                                                sems.at[half]))
        return copies

    @pl.when(jnp.logical_and(j == 0, i == 0))
    def _():
        for cp in fetch(te_ref[0], 0):
            cp.start()

    @pl.when(new_block)
    def _():
        for cp in fetch(te_ref[i], j):
            cp.wait()
        for half in range(2):
            for r0 in range(0, stage_g.shape[0], CAST_ROWS):
                wb[half][r0:r0 + CAST_ROWS, :] = stage[half][r0:r0 + CAST_ROWS, :].astype(BF16)
        next_expert = nx_ref[i]
        same_pass = next_expert >= 0

        @pl.when(jnp.logical_or(same_pass, j + 1 < nj))
        def _():
            for cp in fetch(jnp.where(same_pass, next_expert, te_ref[0]), jnp.where(same_pass, j, j + 1)):
                cp.start()

    @pl.when(i < na_ref[0])
    def _():
        a = a_ref[...]
        g = jnp.dot(a, wb_g[...], preferred_element_type=F32)
        u = jnp.dot(a, wb_u[...], preferred_element_type=F32)
        o_ref[...] = (g * (1.0 / (1.0 + jnp.exp(-g))) * u).astype(o_ref.dtype)

    @pl.when(i >= na_ref[0])
    def _():
        o_ref[...] = jnp.zeros_like(o_ref)


def _swiglu_matmul(a, w, tile_expert, next_expert, n_active, *, tm, tn, name):
    m, k = a.shape
    n_out = w.shape[2] // 2
    nj, ni = n_out // tn, m // tm
    return pl.pallas_call(
        functools.partial(_mm_swiglu_kernel, tn=tn, nj=nj),
        grid_spec=pltpu.PrefetchScalarGridSpec(
            num_scalar_prefetch=3,
            grid=(nj, ni),
            in_specs=[pl.BlockSpec((tm, k), lambda j, i, te, nx, na: (jnp.minimum(i, na[0] - 1), 0)),
                      pl.BlockSpec(memory_space=pl.ANY)],
            out_specs=pl.BlockSpec((tm, tn), lambda j, i, te, nx, na: (i, j)),
            scratch_shapes=[pltpu.VMEM((k, tn), F32), pltpu.VMEM((k, tn), F32),
                            pltpu.VMEM((k, tn), BF16), pltpu.VMEM((k, tn), BF16),
                            pltpu.SemaphoreType.DMA((2,))],
        ),
        out_shape=jax.ShapeDtypeStruct((m, n_out), BF16),
        compiler_params=_params(("arbitrary", "arbitrary")),
        name=name,
    )(tile_expert, next_expert, n_active, a, w)


def _grouped_matmul(a, w, tile_expert, n_active, *, tm, tn, slab=False, out_dtype=F32, name="mm"):
    m, k = a.shape
    n_out = w.shape[2]
    nj, ni = n_out // tn, m // tm

    def a_map(j, i, te, na):
        return (jnp.minimum(i, na[0] - 1), 0)

    w_mode = dict(pipeline_mode=pl.Buffered(1)) if slab else {}
    in_specs = [pl.BlockSpec((tm, k), a_map),
                pl.BlockSpec((None, k, tn), lambda j, i, te, na: (te[i], 0, j), **w_mode)]
    args = [a, w]
    body = functools.partial(_mm_plain_kernel, slab=slab)
    if slab:
        assert nj == 1 and out_dtype == F32
        nsub = n_out // LANES
        out_shape = jax.ShapeDtypeStruct((m * nsub, LANES), F32)
        out_spec = pl.BlockSpec((tm * nsub, LANES), lambda j, i, te, na: (i, 0))
    else:
        out_shape = jax.ShapeDtypeStruct((m, n_out), out_dtype)
        out_spec = pl.BlockSpec((tm, tn), lambda j, i, te, na: (i, j))
    return pl.pallas_call(
        body,
        grid_spec=pltpu.PrefetchScalarGridSpec(
            num_scalar_prefetch=2,
            grid=(nj, ni),
            in_specs=in_specs,
            out_specs=out_spec,
        ),
        out_shape=out_shape,
        compiler_params=_params(("arbitrary", "arbitrary")),
        name=name,
    )(tile_expert, n_active, *args)


def _dense_matmul(a, w, *, tm, tn, group=0, swiglu=False, out_dtype=F32, name="mm"):
    ni = a.shape[0] // tm
    if w.ndim == 2:
        w = w[None]
    groups = jnp.full((ni,), group, jnp.int32)
    n_active = jnp.full((1,), ni, jnp.int32)
    if swiglu:
        return _swiglu_matmul(a, w, groups, jnp.full((ni,), -1, jnp.int32), n_active, tm=tm, tn=tn, name=name)
    return _grouped_matmul(a, w, groups, n_active, tm=tm, tn=tn, out_dtype=out_dtype, name=name)


def _qkv_kernel(a_ref, w_ref, g_ref, cos_ref, sin_ref, o_ref, *, n_norm_tiles, hd, sub):
    j = pl.program_id(0)
    tm = a_ref.shape[0]

    def run(norm):
        for r0 in range(0, tm, sub):
            acc = jnp.dot(a_ref[r0:r0 + sub, :], w_ref[...], preferred_element_type=F32)
            if not norm:
                o_ref[r0:r0 + sub, :] = acc.astype(o_ref.dtype)
                continue
            cos = cos_ref[r0:r0 + sub, :]
            sin = sin_ref[r0:r0 + sub, :]
            even = (lax.broadcasted_iota(jnp.int32, cos.shape, 1) % 2) == 0
            for c in range(acc.shape[1] // hd):
                x = acc[:, c * hd:(c + 1) * hd]
                x = x * lax.rsqrt(jnp.mean(x * x, axis=-1, keepdims=True) + EPS) * g_ref[:, c * hd:(c + 1) * hd]
                partner = jnp.where(even, pltpu.roll(x, hd - 1, axis=1), pltpu.roll(x, 1, axis=1))
                o_ref[r0:r0 + sub, c * hd:(c + 1) * hd] = (x * cos + partner * sin).astype(o_ref.dtype)

    pl.when(j < n_norm_tiles)(lambda: run(True))
    pl.when(j >= n_norm_tiles)(lambda: run(False))


def _qkv_proj(cfg, h, w_qkv, gain, cos_t, sin_t):
    n, d = h.shape
    nq = w_qkv.shape[1]
    hd, nh, kvh = cfg["hd"], cfg["NH"], cfg["KVH"]
    tm = cfg["tm_rows"]
    tn = _pick(kvh * hd, (512, 256, 128))
    tiles_per_batch = cfg["T"] // tm
    return pl.pallas_call(
        functools.partial(_qkv_kernel, n_norm_tiles=(nh + kvh) * hd // tn, hd=hd, sub=min(tm, 128)),
        grid=(nq // tn, n // tm),
        in_specs=[
            pl.BlockSpec((tm, d), lambda j, i: (i, 0)),
            pl.BlockSpec((d, tn), lambda j, i: (0, j)),
            pl.BlockSpec((1, tn), lambda j, i: (0, j)),
            pl.BlockSpec((tm, hd), lambda j, i: (i % tiles_per_batch, 0)),
            pl.BlockSpec((tm, hd), lambda j, i: (i % tiles_per_batch, 0)),
        ],
        out_specs=pl.BlockSpec((tm, tn), lambda j, i: (i, j)),
        out_shape=jax.ShapeDtypeStruct((n, nq), BF16),
        compiler_params=_params(("arbitrary", "arbitrary")),
        name="qkv_proj",
    )(h, w_qkv, gain, cos_t, sin_t)


def _attn_kernel(q_ref, k_ref, v_ref, o_ref, s_ref, vx_ref, *, grp, hd, ctx_len, ctx_tiles, kv_chunk):
    t = pl.program_id(2)
    tq = q_ref.shape[0]

    @pl.when(t == 0)
    def _():
        lane = lax.broadcasted_iota(jnp.int32, (v_ref.shape[0], hd), 1)
        vx_ref[:, :hd] = v_ref[...]
        vx_ref[:, hd:] = jnp.where(lane == 0, 1.0, 0.0).astype(vx_ref.dtype)

    def attend(kv_len):
        chunks = [(s, min(kv_chunk, kv_len - s)) for s in range(0, kv_len, kv_chunk)]

        def scores(g):
            q = q_ref[:, g * hd:(g + 1) * hd]
            m = None
            for s0, sz in chunks:
                s = lax.dot_general(q, k_ref[s0:s0 + sz, :], (((1,), (1,)), ((), ())),
                                    preferred_element_type=F32)
                s_ref[g * tq:(g + 1) * tq, s0:s0 + sz] = s
                for c in range(sz // LANES):
                    piece = s[:, c * LANES:(c + 1) * LANES]
                    m = piece if m is None else jnp.maximum(m, piece)
            return jnp.max(m, axis=-1, keepdims=True)

        def mix(g, m):
            acc = jnp.zeros((tq, 2 * hd), F32)
            for s0, sz in chunks:
                p = jnp.exp2(s_ref[g * tq:(g + 1) * tq, s0:s0 + sz] - m)
                acc = acc + jnp.dot(p.astype(BF16), vx_ref[s0:s0 + sz, :], preferred_element_type=F32)
            o = acc[:, :hd] * (1.0 / acc[:, hd:hd + 1])
            o_ref[:, g * hd:(g + 1) * hd] = o.astype(o_ref.dtype)

        row_max = [scores(g) for g in range(grp)]
        for g in range(grp):
            mix(g, row_max[g])

    pl.when(t < ctx_tiles)(lambda: attend(ctx_len))
    pl.when(t >= ctx_tiles)(lambda: attend(k_ref.shape[0]))


def _attention(cfg, qkv):
    n = qkv.shape[0]
    hd, nh, kvh, t_len = cfg["hd"], cfg["NH"], cfg["KVH"], cfg["T"]
    grp = nh // kvh
    tq = cfg["tq"]
    tiles_per_batch = t_len // tq
    return pl.pallas_call(
        functools.partial(_attn_kernel, grp=grp, hd=hd, ctx_len=cfg["CTX"], ctx_tiles=cfg["CTX"] // tq,
                          kv_chunk=512),
        grid=(cfg["B"], kvh, tiles_per_batch),
        in_specs=[
            pl.BlockSpec((tq, grp * hd), lambda b, kh, t: (b * tiles_per_batch + t, kh)),
            pl.BlockSpec((t_len, hd), lambda b, kh, t: (b, nh + kh)),
            pl.BlockSpec((t_len, hd), lambda b, kh, t: (b, nh + kvh + kh)),
        ],
        out_specs=pl.BlockSpec((tq, grp * hd), lambda b, kh, t: (b * tiles_per_batch + t, kh)),
        out_shape=jax.ShapeDtypeStruct((n, nh * hd), BF16),
        scratch_shapes=[pltpu.VMEM((grp * tq, t_len), F32), pltpu.VMEM((t_len, 2 * hd), BF16)],
        compiler_params=_params(("arbitrary", "arbitrary", "arbitrary")),
        name="attention",
    )(qkv, qkv, qkv)


def _convgate_kernel(b_ref, c_ref, u_ref, cp_ref, up_ref, cn_ref, un_ref, w_ref, o_ref, *,
                     tiles_per_batch, ctx_tiles):
    i = pl.program_id(0)
    tm = b_ref.shape[0]
    tpos = i % tiles_per_batch
    has_prev = jnp.logical_and(tpos != 0, tpos != ctx_tiles)
    has_next = jnp.logical_and(tpos != ctx_tiles - 1, tpos != tiles_per_batch - 1)
    v = c_ref[...] * u_ref[...]
    v_before = jnp.where(has_prev, cp_ref[SUBLANES - 1:SUBLANES, :] * up_ref[SUBLANES - 1:SUBLANES, :], 0.0)
    v_after = jnp.where(has_next, cn_ref[0:1, :] * un_ref[0:1, :], 0.0)
    row = lax.broadcasted_iota(jnp.int32, v.shape, 0)
    v_prev = jnp.where(row == 0, v_before, pltpu.roll(v, 1, axis=0))
    v_next = jnp.where(row == tm - 1, v_after, pltpu.roll(v, tm - 1, axis=0))
    y = v_prev * w_ref[0:1, :] + v * w_ref[1:2, :] + v_next * w_ref[2:3, :]
    o_ref[...] = (b_ref[...] * y).astype(o_ref.dtype)


def _convgate(cfg, z, conv_w):
    n, d3 = z.shape
    d = d3 // 3
    tm = cfg["tm_rows"]
    sub_per_tile = tm // SUBLANES
    last_sub = n // SUBLANES - 1
    tiles_per_batch = cfg["T"] // tm

    def prev_map(col):
        return lambda i: (jnp.maximum(i * sub_per_tile - 1, 0), col)

    def next_map(col):
        return lambda i: (jnp.minimum((i + 1) * sub_per_tile, last_sub), col)

    return pl.pallas_call(
        functools.partial(_convgate_kernel, tiles_per_batch=tiles_per_batch, ctx_tiles=cfg["CTX"] // tm),
        grid=(n // tm,),
        in_specs=[
            pl.BlockSpec((tm, d), lambda i: (i, 0)),
            pl.BlockSpec((tm, d), lambda i: (i, 1)),
            pl.BlockSpec((tm, d), lambda i: (i, 2)),
            pl.BlockSpec((SUBLANES, d), prev_map(1)),
            pl.BlockSpec((SUBLANES, d), prev_map(2)),
            pl.BlockSpec((SUBLANES, d), next_map(1)),
            pl.BlockSpec((SUBLANES, d), next_map(2)),
            pl.BlockSpec((conv_w.shape[0], d), lambda i: (0, 0)),
        ],
        out_specs=pl.BlockSpec((tm, d), lambda i: (i, 0)),
        out_shape=jax.ShapeDtypeStruct((n, d), BF16),
        compiler_params=_params(("arbitrary",)),
        name="convgate",
    )(z, z, z, z, z, z, z, conv_w)


def _route_kernel(lg_ref, meta_ref, wts_ref, cnt_ref, carry_ref, *, n_experts):
    i = pl.program_id(0)

    @pl.when(i == 0)
    def _():
        carry_ref[...] = jnp.zeros_like(carry_ref)

    tm = lg_ref.shape[0]
    logits = lg_ref[...]
    lane = lax.broadcasted_iota(jnp.int32, logits.shape, 1)
    neg = jnp.float32(-jnp.inf)
    lg = jnp.where(lane < n_experts, logits, neg)
    v1 = jnp.max(lg, axis=-1, keepdims=True)
    i1 = jnp.min(jnp.where(lg == v1, lane, LANES), axis=-1, keepdims=True)
    lg2 = jnp.where(lane == i1, neg, lg)
    v2 = jnp.max(lg2, axis=-1, keepdims=True)
    i2 = jnp.min(jnp.where(lg2 == v2, lane, LANES), axis=-1, keepdims=True)
    e = jnp.exp(v2 - v1)
    w1 = 1.0 / (1.0 + e)
    w2 = e / (1.0 + e)
    sel1 = lane == i1
    sel2 = lane == i2
    onehot = jnp.where(jnp.logical_or(sel1, sel2), 1.0, 0.0)
    rr = lax.broadcasted_iota(jnp.int32, (tm, tm), 0)
    cc = lax.broadcasted_iota(jnp.int32, (tm, tm), 1)
    tri = jnp.where(rr > cc, 1.0, 0.0).astype(BF16)
    before = jnp.dot(tri, onehot.astype(BF16), preferred_element_type=F32) + carry_ref[...]
    r1 = jnp.sum(jnp.where(sel1, before, 0.0), axis=-1, keepdims=True).astype(jnp.int32)
    r2 = jnp.sum(jnp.where(sel2, before, 0.0), axis=-1, keepdims=True).astype(jnp.int32)
    carry_ref[...] = carry_ref[...] + jnp.sum(onehot, axis=0, keepdims=True)
    meta_ref[...] = jnp.where(lane == 0, i1, jnp.where(lane == 1, i2, jnp.where(lane == 2, r1,
                              jnp.where(lane == 3, r2, 0))))
    wts_ref[...] = jnp.where(lane == 0, w1, jnp.where(lane == 1, w2, 0.0))
    cnt_ref[...] = jnp.broadcast_to(carry_ref[...], cnt_ref.shape)


def _route(cfg, logits, n_experts):
    n = logits.shape[0]
    tm = cfg["tm_rows"]
    return pl.pallas_call(
        functools.partial(_route_kernel, n_experts=n_experts),
        grid=(n // tm,),
        in_specs=[pl.BlockSpec((tm, LANES), lambda i: (i, 0))],
        out_specs=[pl.BlockSpec((tm, LANES), lambda i: (i, 0)), pl.BlockSpec((tm, LANES), lambda i: (i, 0)),
                   pl.BlockSpec((SUBLANES, LANES), lambda i: (0, 0))],
        out_shape=[jax.ShapeDtypeStruct((n, LANES), jnp.int32), jax.ShapeDtypeStruct((n, LANES), F32),
                   jax.ShapeDtypeStruct((SUBLANES, LANES), F32)],
        scratch_shapes=[pltpu.VMEM((1, LANES), F32)],
        compiler_params=_params(("arbitrary",)),
        name="route",
    )(logits)


GATHER_PITCH_PAD = SUBLANES


def _slab_gather_kernel(idx0_ref, idxn_ref, src_ref, *rest, rows, per_row, nsub, steps, weighted):
    if weighted:
        w_ref, o_ref, buf, sems = rest
    else:
        o_ref, buf, sems = rest
    i = pl.program_id(0)
    slot = i % 2
    pitch = nsub + GATHER_PITCH_PAD
    n_slabs = per_row * rows
    unroll = 8

    def slab_copy(src_slab, dst_slab, sl):
        src0 = pl.multiple_of(src_slab * nsub, nsub)
        dst0 = pl.multiple_of(dst_slab * pitch, SUBLANES)
        return pltpu.make_async_copy(src_ref.at[pl.ds(src0, nsub)], buf.at[sl, pl.ds(dst0, nsub)], sems.at[sl])

    def issue(idx_ref, sl):
        def body(o, carry):
            for u in range(unroll):
                r = o * unroll + u
                slab_copy(idx_ref[0, r], r, sl).start(priority=u % 2)
            return carry

        lax.fori_loop(0, n_slabs // unroll, body, 0)

    def drain(sl):
        def body(o, carry):
            for u in range(unroll):
                slab_copy(0, 0, sl).wait()
            return carry

        lax.fori_loop(0, n_slabs // unroll, body, 0)

    @pl.when(i == 0)
    def _():
        issue(idx0_ref, 0)

    @pl.when(i + 1 < steps)
    def _():
        issue(idxn_ref, 1 - slot)

    drain(slot)

    if weighted:
        wk = [jnp.broadcast_to(w_ref[:, k:k + 1], (rows, LANES)) for k in range(per_row)]

    for sl in range(2):
        @pl.when(slot == sl)
        def _():
            for s in range(nsub):
                parts = [buf[sl, pl.ds(k * rows * pitch + s, rows, stride=pitch), :] for k in range(per_row)]
                if weighted:
                    v = wk[0] * parts[0]
                    for k in range(1, per_row):
                        v = v + wk[k] * parts[k]
                else:
                    (v,) = parts
                o_ref[:, s * LANES:(s + 1) * LANES] = v.astype(o_ref.dtype)


def _slab_gather(src_slabs, idx, *, d, rows, per_row=1, wts=None, out_dtype=F32, name="slab_gather"):
    n_out = idx.shape[1]
    nsub = d // LANES
    steps = n_out // rows
    pitch = nsub + GATHER_PITCH_PAD
    idx3 = idx.reshape(per_row, steps, rows).transpose(1, 0, 2).reshape(steps, 1, per_row * rows)
    smem_idx = lambda imap: pl.BlockSpec((None, 1, per_row * rows), imap, memory_space=pltpu.SMEM)
    in_specs = [smem_idx(lambda i: (0, 0, 0)),
                smem_idx(lambda i: (jnp.minimum(i + 1, steps - 1), 0, 0)),
                pl.BlockSpec(memory_space=pl.ANY)]
    args = [idx3, idx3, src_slabs]
    if wts is not None:
        in_specs.append(pl.BlockSpec((rows, LANES), lambda i: (i, 0)))
        args.append(wts)
    return pl.pallas_call(
        functools.partial(_slab_gather_kernel, rows=rows, per_row=per_row, nsub=nsub, steps=steps,
                          weighted=wts is not None),
        grid=(steps,),
        in_specs=in_specs,
        out_specs=pl.BlockSpec((rows, d), lambda i: (i, 0)),
        out_shape=jax.ShapeDtypeStruct((n_out, d), out_dtype),
        scratch_shapes=[pltpu.VMEM((2, per_row * rows * pitch, LANES), F32), pltpu.SemaphoreType.DMA((2,))],
        compiler_params=_params(("arbitrary",)),
        name=name,
    )(*args)


def _moe(cfg, logits, h_slabs, n_exp, layer, w_gu, w_down):
    n = logits.shape[0]
    d = w_gu.shape[1]
    tm = cfg["tm_moe"]
    meta, wts, cnt = _route(cfg, logits, n_exp)
    counts = cnt[0, :n_exp].astype(jnp.int32)
    tiles_e = (counts + tm - 1) // tm
    tiles_cum = jnp.cumsum(tiles_e)
    row_off = (tiles_cum - tiles_e) * tm
    pos = jnp.stack([row_off[meta[:, 0]] + meta[:, 2], row_off[meta[:, 1]] + meta[:, 3]])
    n_tiles = (2 * n) // tm + n_exp
    tok = jnp.arange(n, dtype=jnp.int32)
    src = jnp.zeros((n_tiles * tm,), jnp.int32).at[pos.reshape(-1)].set(jnp.concatenate([tok, tok]))
    n_active = tiles_cum[-1:]
    tile_ids = jnp.minimum(jnp.arange(n_tiles, dtype=jnp.int32), n_active[0] - 1)
    tile_expert = jnp.minimum(jnp.sum(tiles_cum[None, :] <= tile_ids[:, None], axis=1), n_exp - 1).astype(jnp.int32)
    experts = jnp.arange(n_exp, dtype=jnp.int32)
    later = jnp.logical_and(experts[None, :] > experts[:, None], tiles_e[None, :] > 0)
    next_present = jnp.min(jnp.where(later, experts[None, :], n_exp), axis=1)
    next_expert = next_present[tile_expert]
    next_expert = jnp.where(next_expert < n_exp, next_expert + layer * n_exp, -1).astype(jnp.int32)
    tile_expert = tile_expert + layer * n_exp

    xs = _slab_gather(h_slabs, src[None], d=d, rows=cfg["rows_dispatch"], out_dtype=BF16, name="moe_dispatch")
    mid = _swiglu_matmul(xs, w_gu, tile_expert, next_expert, n_active, tm=tm, tn=cfg["tn_ff"], name="moe_up")
    out_slabs = _grouped_matmul(mid, w_down, tile_expert, n_active, tm=tm, tn=d, slab=True, name="moe_down")
    return _slab_gather(out_slabs, pos, d=d, rows=cfg["rows_combine"], per_row=2, wts=wts, name="moe_combine")


def _rope_tables(cfg):
    hd, seq, ctx = cfg["hd"], cfg["SEQ"], cfg["CTX"]
    rows = seq // GRID_W
    r, col = jnp.meshgrid(jnp.arange(rows), jnp.arange(GRID_W), indexing="ij")
    r = r.reshape(-1).astype(F32)
    col = col.reshape(-1).astype(F32)
    n_freq = hd // 4
    inv = ROPE_THETA ** (-jnp.arange(n_freq, dtype=F32) / n_freq)
    ang = jnp.concatenate([r[:, None] * inv, col[:, None] * inv], axis=-1)
    cos = jnp.repeat(jnp.cos(ang), 2, axis=-1)
    sin = jnp.repeat(jnp.sin(ang), 2, axis=-1) * jnp.tile(jnp.array([-1.0, 1.0], F32), hd // 2)
    cos = jnp.concatenate([jnp.ones((ctx, hd), F32), cos], axis=0)
    sin = jnp.concatenate([jnp.zeros((ctx, hd), F32), sin], axis=0)
    return cos, sin


def kernel(x, c, ctx, c_ctx, ada_w, ada_b, norm_g, attn_w_qkv, attn_q_norm, attn_k_norm, attn_w_o,
           conv_w_in, conv_w, conv_w_out, ffn_w_gu, ffn_w_down, moe_router, moe_w_gu, moe_w_down):
    nb, seq, d = x.shape
    ctx_len = ctx.shape[1]
    depth = ada_w.shape[0]
    hd = attn_q_norm.shape[1]
    nh = attn_w_o.shape[1] // hd
    kvh = (attn_w_qkv.shape[2] // hd - nh) // 2
    d_ff = ffn_w_down.shape[1]
    t_len = ctx_len + seq
    tm_rows = _pick(ctx_len, (256, 128))

    def stream_cfg(ctx_rows):
        rows = nb * (ctx_rows + seq)
        return dict(B=nb, SEQ=seq, CTX=ctx_rows, T=ctx_rows + seq, hd=hd, NH=nh, KVH=kvh,
                    R=-(-(nb + 1) // SUBLANES) * SUBLANES, tm_rows=tm_rows, tq=_pick(ctx_len, (256, 128)),
                    tm_mm=_pick(rows, (512, 256, 128)), tm_moe=_pick(2 * rows, (512, 256, 128)),
                    tn_ff=_pick(d_ff, (1408, 512, 256, 128)), tn_down=_pick(d, (512, 256, 128)),
                    rows_dispatch=256, rows_combine=128)

    cfg = stream_cfg(ctx_len)
    ctx_free_from = depth - 1 if (depth - 1) % 2 == 1 else depth
    n = nb * t_len

    xa = jnp.concatenate([ctx, x], axis=1).reshape(n, d)
    cvec = jnp.zeros((cfg["R"], d), F32).at[:nb].set(c).at[nb].set(c_ctx)
    mod = _modulation(cvec, ada_w, ada_b)
    mod = mod.reshape(depth, cfg["R"], 6, d).transpose(0, 2, 1, 3).reshape(depth * 6 * cfg["R"], 1, d)
    cos_t, sin_t = _rope_tables(cfg)

    w_qkv = attn_w_qkv.astype(BF16)
    w_o = attn_w_o.astype(BF16)
    w_cin = conv_w_in.astype(BF16)
    w_cout = conv_w_out.astype(BF16)
    w_dn = ffn_w_down.astype(BF16)
    n_exp = moe_router.shape[2]
    m_gu = moe_w_gu.reshape((-1,) + moe_w_gu.shape[2:])
    m_dn = moe_w_down.astype(BF16).reshape((-1,) + moe_w_down.shape[2:])

    (h,) = _resnorm(cfg, xa, mod, g_pre=norm_g[0, 0], shift_idx=(0, 0), scale_idx=(0, 1))
    for i in range(depth):
        j = i // 2
        moe_layer = i % 2 == 1
        tm = cfg["tm_mm"]
        if i % 2 == 0:
            q_scale = (hd ** -0.5) * 1.4426950408889634
            gain = jnp.concatenate([jnp.tile(attn_q_norm[j] * q_scale, nh), jnp.tile(attn_k_norm[j], kvh),
                                    jnp.ones((kvh * hd,), F32)]).reshape(1, -1)
            qkv = _qkv_proj(cfg, h, w_qkv[j], gain, cos_t, sin_t)
            o = _attention(cfg, qkv)
            y = _dense_matmul(o, w_o, group=j, tm=tm, tn=cfg["tn_down"], name="attn_out")
        else:
            z = _dense_matmul(h, w_cin, group=j, tm=tm, tn=_pick(3 * d, (1024, 512, 256, 128)), name="conv_in")
            gated = _convgate(cfg, z, conv_w[j])
            y = _dense_matmul(gated, w_cout, group=j, tm=tm, tn=cfg["tn_down"], name="conv_out")
        router = jnp.zeros((d, LANES), F32).at[:, :n_exp].set(moe_router[j]) if moe_layer else None
        res = _resnorm(cfg, xa, mod, y=y, g_post=norm_g[i, 1], gate_idx=(i, 2), g_pre=norm_g[i, 2],
                       shift_idx=(i, 3), scale_idx=(i, 4), router=router)
        xa, h = res[0], res[1]
        last = i == depth - 1
        nxt = {} if last else dict(g_pre=norm_g[i + 1, 0], shift_idx=(i + 1, 0), scale_idx=(i + 1, 1))
        if moe_layer:
            y = _moe(cfg, res[1], res[2], n_exp, j, m_gu, m_dn)
        else:
            mid = _dense_matmul(h, ffn_w_gu, group=j, tm=tm, tn=cfg["tn_ff"], swiglu=True, out_dtype=BF16,
                                name="ffn_up")
            y = _dense_matmul(mid, w_dn, group=j, tm=tm, tn=cfg["tn_down"], name="ffn_down")
        drop_ctx = cfg["CTX"] > 0 and (last or i + 1 >= ctx_free_from)
        res = _resnorm(cfg, xa, mod, y=y, g_post=norm_g[i, 3], gate_idx=(i, 5), latent_only=drop_ctx, **nxt)
        if drop_ctx:
            cfg = stream_cfg(0)
        xa = res[0]
        if not last:
            h = res[1]
    return xa.reshape(nb, seq, d)
```

```python
import functools

import jax
import jax.numpy as jnp
from jax import lax
from jax.experimental import pallas as pl
from jax.experimental.pallas import tpu as pltpu

EPS = 1e-6
GRID_W = 64
ROPE_THETA = 10000.0
LANES = 128
SUBLANES = 8
VMEM_LIMIT = 56 * 1024 * 1024

F32 = jnp.float32
BF16 = jnp.bfloat16


def _params(sem, vmem=VMEM_LIMIT):
    return pltpu.CompilerParams(dimension_semantics=sem, vmem_limit_bytes=vmem)


def _pick(n, candidates):
    for c in candidates:
        if n % c == 0:
            return c
    raise ValueError(f"no tile in {candidates} divides {n}")


def _mod_kernel(c_ref, w_ref, b_ref, o_ref):
    c = c_ref[...]
    s = c * (1.0 / (1.0 + jnp.exp(-c)))
    o_ref[...] = jnp.dot(s.astype(BF16), w_ref[...].astype(BF16), preferred_element_type=F32) + b_ref[...]


def _modulation(cvec, ada_w, ada_b):
    depth, d, n = ada_w.shape
    r = cvec.shape[0]
    tn = _pick(n, (1024, 512, 256, 128))
    return pl.pallas_call(
        _mod_kernel,
        grid=(depth, n // tn),
        in_specs=[
            pl.BlockSpec((r, d), lambda l, j: (0, 0)),
            pl.BlockSpec((None, d, tn), lambda l, j: (l, 0, j)),
            pl.BlockSpec((None, 1, tn), lambda l, j: (l, 0, j)),
        ],
        out_specs=pl.BlockSpec((None, r, tn), lambda l, j: (l, 0, j)),
        out_shape=jax.ShapeDtypeStruct((depth, r, n), F32),
        compiler_params=_params(("arbitrary", "arbitrary")),
        name="ada_mod",
    )(cvec, ada_w, ada_b.reshape(depth, 1, n))


def _rms(x, g):
    return x * lax.rsqrt(jnp.mean(x * x, axis=-1, keepdims=True) + EPS) * g


def _resnorm_kernel(*refs, n_y, has_h, slab):
    it = iter(refs)
    x = next(it)[...]
    if n_y:
        y = next(it)[...]
        g_post = next(it)[...]
        gate = next(it)[...]
        x = x + gate * _rms(y, g_post)
    if has_h:
        g_pre = next(it)[...]
        shift = next(it)[...]
        scale = next(it)[...]
        if slab:
            router_ref = next(it)
    if n_y:
        next(it)[...] = x
    if has_h:
        h = _rms(x, g_pre) * (1.0 + scale) + shift
        if slab:
            next(it)[...] = jnp.dot(h, router_ref[...], preferred_element_type=F32, precision=lax.Precision.HIGHEST)
            _store_slabs(next(it), h)
        else:
            h_ref = next(it)
            h_ref[...] = h.astype(h_ref.dtype)


def _store_slabs(slab_ref, v):
    rows, d = v.shape
    nsub = d // LANES
    for s in range(nsub):
        slab_ref[pl.ds(s, rows, stride=nsub), :] = v[:, s * LANES:(s + 1) * LANES]


def _resnorm(cfg, x, mod, *, y=None, g_post=None, gate_idx=None, g_pre=None,
             shift_idx=None, scale_idx=None, router=None, latent_only=False):
    d = x.shape[1]
    tm = cfg["tm_rows"]
    tiles_per_batch = cfg["T"] // tm
    ctx_tiles = cfg["CTX"] // tm
    nb, r = cfg["B"], cfg["R"]
    n_y = 0 if y is None else 1
    has_h = g_pre is not None
    slab = router is not None
    out_tiles_per_batch = tiles_per_batch - ctx_tiles if latent_only else tiles_per_batch
    n = nb * out_tiles_per_batch * tm

    def in_tile(i):
        if latent_only:
            return (i // out_tiles_per_batch) * tiles_per_batch + ctx_tiles + i % out_tiles_per_batch
        return i

    def mod_spec(idx):
        l, k = idx

        def imap(i):
            it = in_tile(i)
            row = jnp.where(it % tiles_per_batch < ctx_tiles, nb, it // tiles_per_batch)
            return ((l * 6 + k) * r + row, 0, 0)

        return pl.BlockSpec((None, 1, d), imap)

    row_spec = pl.BlockSpec((tm, d), lambda i: (i, 0))
    in_row_spec = pl.BlockSpec((tm, d), lambda i: (in_tile(i), 0))
    vec_spec = pl.BlockSpec((1, d), lambda i: (0, 0))
    args, specs = [x], [in_row_spec]
    if n_y:
        args += [y, g_post.reshape(1, d), mod]
        specs += [in_row_spec, vec_spec, mod_spec(gate_idx)]
    if has_h:
        args += [g_pre.reshape(1, d), mod, mod]
        specs += [vec_spec, mod_spec(shift_idx), mod_spec(scale_idx)]
        if slab:
            args.append(router)
            specs.append(pl.BlockSpec((d, LANES), lambda i: (0, 0)))
    out_shape, out_specs = [], []
    if n_y:
        out_shape.append(jax.ShapeDtypeStruct((n, d), F32))
        out_specs.append(row_spec)
    if has_h and slab:
        nsub = d // LANES
        out_shape += [jax.ShapeDtypeStruct((n, LANES), F32), jax.ShapeDtypeStruct((n * nsub, LANES), F32)]
        out_specs += [pl.BlockSpec((tm, LANES), lambda i: (i, 0)), pl.BlockSpec((tm * nsub, LANES), lambda i: (i, 0))]
    elif has_h:
        out_shape.append(jax.ShapeDtypeStruct((n, d), BF16))
        out_specs.append(row_spec)
    return pl.pallas_call(
        functools.partial(_resnorm_kernel, n_y=n_y, has_h=has_h, slab=slab),
        grid=(n // tm,),
        in_specs=specs,
        out_specs=out_specs,
        out_shape=out_shape,
        compiler_params=_params(("arbitrary",)),
        name="resnorm",
    )(*args)


def _mm_plain_kernel(te_ref, na_ref, a_ref, w_ref, o_ref, *, slab):
    i = pl.program_id(1)

    @pl.when(i < na_ref[0])
    def _():
        acc = jnp.dot(a_ref[...], w_ref[...], preferred_element_type=F32)
        if slab:
            _store_slabs(o_ref, acc)
        else:
            o_ref[...] = acc.astype(o_ref.dtype)

    @pl.when(i >= na_ref[0])
    def _():
        o_ref[...] = jnp.zeros_like(o_ref)


CAST_ROWS = 256


def _mm_swiglu_kernel(te_ref, nx_ref, na_ref, a_ref, w_hbm, o_ref, stage_g, stage_u, wb_g, wb_u, sems, *,
                      tn, nj):
    j = pl.program_id(0)
    i = pl.program_id(1)
    new_block = jnp.logical_or(i == 0, te_ref[i] != te_ref[jnp.maximum(i - 1, 0)])
    stage = (stage_g, stage_u)
    wb = (wb_g, wb_u)

    def fetch(expert, col_tile):
        copies = []
        for half in range(2):
            col0 = pl.multiple_of((col_tile + half * nj) * tn, LANES)
            copies.append(pltpu.make_async_copy(w_hbm.at[expert, :, pl.ds(col0, tn)], stage[half],
                                                sems.at[half]))
        return copies

    @pl.when(jnp.logical_and(j == 0, i == 0))
    def _():
        for cp in fetch(te_ref[0], 0):
            cp.start()

    @pl.when(new_block)
    def _():
        for cp in fetch(te_ref[i], j):
            cp.wait()
        for half in range(2):
            for r0 in range(0, stage_g.shape[0], CAST_ROWS):
                wb[half][r0:r0 + CAST_ROWS, :] = stage[half][r0:r0 + CAST_ROWS, :].astype(BF16)
        next_expert = nx_ref[i]
        same_pass = next_expert >= 0

        @pl.when(jnp.logical_or(same_pass, j + 1 < nj))
        def _():
            for cp in fetch(jnp.where(same_pass, next_expert, te_ref[0]), jnp.where(same_pass, j, j + 1)):
                cp.start()

    @pl.when(i < na_ref[0])
    def _():
        a = a_ref[...]
        g = jnp.dot(a, wb_g[...], preferred_element_type=F32)
        u = jnp.dot(a, wb_u[...], preferred_element_type=F32)
        o_ref[...] = (g * (1.0 / (1.0 + jnp.exp(-g))) * u).astype(o_ref.dtype)

    @pl.when(i >= na_ref[0])
    def _():
        o_ref[...] = jnp.zeros_like(o_ref)


def _swiglu_matmul(a, w, tile_expert, next_expert, n_active, *, tm, tn, name):
    m, k = a.shape
    n_out = w.shape[2] // 2
    nj, ni = n_out // tn, m // tm
    return pl.pallas_call(
        functools.partial(_mm_swiglu_kernel, tn=tn, nj=nj),
        grid_spec=pltpu.PrefetchScalarGridSpec(
            num_scalar_prefetch=3,
            grid=(nj, ni),
            in_specs=[pl.BlockSpec((tm, k), lambda j, i, te, nx, na: (jnp.minimum(i, na[0] - 1), 0)),
                      pl.BlockSpec(memory_space=pl.ANY)],
            out_specs=pl.BlockSpec((tm, tn), lambda j, i, te, nx, na: (i, j)),
            scratch_shapes=[pltpu.VMEM((k, tn), F32), pltpu.VMEM((k, tn), F32),
                            pltpu.VMEM((k, tn), BF16), pltpu.VMEM((k, tn), BF16),
                            pltpu.SemaphoreType.DMA((2,))],
        ),
        out_shape=jax.ShapeDtypeStruct((m, n_out), BF16),
        compiler_params=_params(("arbitrary", "arbitrary")),
        name=name,
    )(tile_expert, next_expert, n_active, a, w)


def _grouped_matmul(a, w, tile_expert, n_active, *, tm, tn, slab=False, out_dtype=F32, name="mm"):
    m, k = a.shape
    n_out = w.shape[2]
    nj, ni = n_out // tn, m // tm

    def a_map(j, i, te, na):
        return (jnp.minimum(i, na[0] - 1), 0)

    w_mode = dict(pipeline_mode=pl.Buffered(1)) if slab else {}
    in_specs = [pl.BlockSpec((tm, k), a_map),
                pl.BlockSpec((None, k, tn), lambda j, i, te, na: (te[i], 0, j), **w_mode)]
    args = [a, w]
    body = functools.partial(_mm_plain_kernel, slab=slab)
    if slab:
        assert nj == 1 and out_dtype == F32
        nsub = n_out // LANES
        out_shape = jax.ShapeDtypeStruct((m * nsub, LANES), F32)
        out_spec = pl.BlockSpec((tm * nsub, LANES), lambda j, i, te, na: (i, 0))
    else:
        out_shape = jax.ShapeDtypeStruct((m, n_out), out_dtype)
        out_spec = pl.BlockSpec((tm, tn), lambda j, i, te, na: (i, j))
    return pl.pallas_call(
        body,
        grid_spec=pltpu.PrefetchScalarGridSpec(
            num_scalar_prefetch=2,
            grid=(nj, ni),
            in_specs=in_specs,
            out_specs=out_spec,
        ),
        out_shape=out_shape,
        compiler_params=_params(("arbitrary", "arbitrary")),
        name=name,
    )(tile_expert, n_active, *args)


def _dense_matmul(a, w, *, tm, tn, group=0, swiglu=False, out_dtype=F32, name="mm"):
    ni = a.shape[0] // tm
    if w.ndim == 2:
        w = w[None]
    groups = jnp.full((ni,), group, jnp.int32)
    n_active = jnp.full((1,), ni, jnp.int32)
    if swiglu:
        return _swiglu_matmul(a, w, groups, jnp.full((ni,), -1, jnp.int32), n_active, tm=tm, tn=tn, name=name)
    return _grouped_matmul(a, w, groups, n_active, tm=tm, tn=tn, out_dtype=out_dtype, name=name)


def _qkv_kernel(a_ref, w_ref, g_ref, cos_ref, sin_ref, o_ref, *, n_norm_tiles, hd, sub):
    j = pl.program_id(0)
    tm = a_ref.shape[0]

    def run(norm):
        for r0 in range(0, tm, sub):
            acc = jnp.dot(a_ref[r0:r0 + sub, :], w_ref[...], preferred_element_type=F32)
            if not norm:
                o_ref[r0:r0 + sub, :] = acc.astype(o_ref.dtype)
                continue
            cos = cos_ref[r0:r0 + sub, :]
            sin = sin_ref[r0:r0 + sub, :]
            even = (lax.broadcasted_iota(jnp.int32, cos.shape, 1) % 2) == 0
            for c in range(acc.shape[1] // hd):
                x = acc[:, c * hd:(c + 1) * hd]
                x = x * lax.rsqrt(jnp.mean(x * x, axis=-1, keepdims=True) + EPS) * g_ref[:, c * hd:(c + 1) * hd]
                partner = jnp.where(even, pltpu.roll(x, hd - 1, axis=1), pltpu.roll(x, 1, axis=1))
                o_ref[r0:r0 + sub, c * hd:(c + 1) * hd] = (x * cos + partner * sin).astype(o_ref.dtype)

    pl.when(j < n_norm_tiles)(lambda: run(True))
    pl.when(j >= n_norm_tiles)(lambda: run(False))


def _qkv_proj(cfg, h, w_qkv, gain, cos_t, sin_t):
    n, d = h.shape
    nq = w_qkv.shape[1]
    hd, nh, kvh = cfg["hd"], cfg["NH"], cfg["KVH"]
    tm = cfg["tm_rows"]
    tn = _pick(kvh * hd, (512, 256, 128))
    tiles_per_batch = cfg["T"] // tm
    return pl.pallas_call(
        functools.partial(_qkv_kernel, n_norm_tiles=(nh + kvh) * hd // tn, hd=hd, sub=min(tm, 128)),
        grid=(nq // tn, n // tm),
        in_specs=[
            pl.BlockSpec((tm, d), lambda j, i: (i, 0)),
            pl.BlockSpec((d, tn), lambda j, i: (0, j)),
            pl.BlockSpec((1, tn), lambda j, i: (0, j)),
            pl.BlockSpec((tm, hd), lambda j, i: (i % tiles_per_batch, 0)),
            pl.BlockSpec((tm, hd), lambda j, i: (i % tiles_per_batch, 0)),
        ],
        out_specs=pl.BlockSpec((tm, tn), lambda j, i: (i, j)),
        out_shape=jax.ShapeDtypeStruct((n, nq), BF16),
        compiler_params=_params(("arbitrary", "arbitrary")),
        name="qkv_proj",
    )(h, w_qkv, gain, cos_t, sin_t)


def _attn_kernel(q_ref, k_ref, v_ref, o_ref, s_ref, vx_ref, *, grp, hd, ctx_len, ctx_tiles, kv_chunk):
    t = pl.program_id(2)
    tq = q_ref.shape[0]

    @pl.when(t == 0)
    def _():
        lane = lax.broadcasted_iota(jnp.int32, (v_ref.shape[0], hd), 1)
        vx_ref[:, :hd] = v_ref[...]
        vx_ref[:, hd:] = jnp.where(lane == 0, 1.0, 0.0).astype(vx_ref.dtype)

    def attend(kv_len):
        chunks = [(s, min(kv_chunk, kv_len - s)) for s in range(0, kv_len, kv_chunk)]

        def scores(g):
            q = q_ref[:, g * hd:(g + 1) * hd]
            m = None
            for s0, sz in chunks:
                s = lax.dot_general(q, k_ref[s0:s0 + sz, :], (((1,), (1,)), ((), ())),
                                    preferred_element_type=F32)
                s_ref[g * tq:(g + 1) * tq, s0:s0 + sz] = s
                for c in range(sz // LANES):
                    piece = s[:, c * LANES:(c + 1) * LANES]
                    m = piece if m is None else jnp.maximum(m, piece)
            return jnp.max(m, axis=-1, keepdims=True)

        def mix(g, m):
            acc = jnp.zeros((tq, 2 * hd), F32)
            for s0, sz in chunks:
                p = jnp.exp2(s_ref[g * tq:(g + 1) * tq, s0:s0 + sz] - m)
                acc = acc + jnp.dot(p.astype(BF16), vx_ref[s0:s0 + sz, :], preferred_element_type=F32)
            o = acc[:, :hd] * (1.0 / acc[:, hd:hd + 1])
            o_ref[:, g * hd:(g + 1) * hd] = o.astype(o_ref.dtype)

        row_max = [scores(g) for g in range(grp)]
        for g in range(grp):
            mix(g, row_max[g])

    pl.when(t < ctx_tiles)(lambda: attend(ctx_len))
    pl.when(t >= ctx_tiles)(lambda: attend(k_ref.shape[0]))


def _attention(cfg, qkv):
    n = qkv.shape[0]
    hd, nh, kvh, t_len = cfg["hd"], cfg["NH"], cfg["KVH"], cfg["T"]
    grp = nh // kvh
    tq = cfg["tq"]
    tiles_per_batch = t_len // tq
    return pl.pallas_call(
        functools.partial(_attn_kernel, grp=grp, hd=hd, ctx_len=cfg["CTX"], ctx_tiles=cfg["CTX"] // tq,
                          kv_chunk=512),
        grid=(cfg["B"], kvh, tiles_per_batch),
        in_specs=[
            pl.BlockSpec((tq, grp * hd), lambda b, kh, t: (b * tiles_per_batch + t, kh)),
            pl.BlockSpec((t_len, hd), lambda b, kh, t: (b, nh + kh)),
            pl.BlockSpec((t_len, hd), lambda b, kh, t: (b, nh + kvh + kh)),
        ],
        out_specs=pl.BlockSpec((tq, grp * hd), lambda b, kh, t: (b * tiles_per_batch + t, kh)),
        out_shape=jax.ShapeDtypeStruct((n, nh * hd), BF16),
        scratch_shapes=[pltpu.VMEM((grp * tq, t_len), F32), pltpu.VMEM((t_len, 2 * hd), BF16)],
        compiler_params=_params(("arbitrary", "arbitrary", "arbitrary")),
        name="attention",
    )(qkv, qkv, qkv)


def _convgate_kernel(b_ref, c_ref, u_ref, cp_ref, up_ref, cn_ref, un_ref, w_ref, o_ref, *,
                     tiles_per_batch, ctx_tiles):
    i = pl.program_id(0)
    tm = b_ref.shape[0]
    tpos = i % tiles_per_batch
    has_prev = jnp.logical_and(tpos != 0, tpos != ctx_tiles)
    has_next = jnp.logical_and(tpos != ctx_tiles - 1, tpos != tiles_per_batch - 1)
    v = c_ref[...] * u_ref[...]
    v_before = jnp.where(has_prev, cp_ref[SUBLANES - 1:SUBLANES, :] * up_ref[SUBLANES - 1:SUBLANES, :], 0.0)
    v_after = jnp.where(has_next, cn_ref[0:1, :] * un_ref[0:1, :], 0.0)
    row = lax.broadcasted_iota(jnp.int32, v.shape, 0)
    v_prev = jnp.where(row == 0, v_before, pltpu.roll(v, 1, axis=0))
    v_next = jnp.where(row == tm - 1, v_after, pltpu.roll(v, tm - 1, axis=0))
    y = v_prev * w_ref[0:1, :] + v * w_ref[1:2, :] + v_next * w_ref[2:3, :]
    o_ref[...] = (b_ref[...] * y).astype(o_ref.dtype)


def _convgate(cfg, z, conv_w):
    n, d3 = z.shape
    d = d3 // 3
    tm = cfg["tm_rows"]
    sub_per_tile = tm // SUBLANES
    last_sub = n // SUBLANES - 1
    tiles_per_batch = cfg["T"] // tm

    def prev_map(col):
        return lambda i: (jnp.maximum(i * sub_per_tile - 1, 0), col)

    def next_map(col):
        return lambda i: (jnp.minimum((i + 1) * sub_per_tile, last_sub), col)

    return pl.pallas_call(
        functools.partial(_convgate_kernel, tiles_per_batch=tiles_per_batch, ctx_tiles=cfg["CTX"] // tm),
        grid=(n // tm,),
        in_specs=[
            pl.BlockSpec((tm, d), lambda i: (i, 0)),
            pl.BlockSpec((tm, d), lambda i: (i, 1)),
            pl.BlockSpec((tm, d), lambda i: (i, 2)),
            pl.BlockSpec((SUBLANES, d), prev_map(1)),
            pl.BlockSpec((SUBLANES, d), prev_map(2)),
            pl.BlockSpec((SUBLANES, d), next_map(1)),
            pl.BlockSpec((SUBLANES, d), next_map(2)),
            pl.BlockSpec((conv_w.shape[0], d), lambda i: (0, 0)),
        ],
        out_specs=pl.BlockSpec((tm, d), lambda i: (i, 0)),
        out_shape=jax.ShapeDtypeStruct((n, d), BF16),
        compiler_params=_params(("arbitrary",)),
        name="convgate",
    )(z, z, z, z, z, z, z, conv_w)


def _route_kernel(lg_ref, meta_ref, wts_ref, cnt_ref, carry_ref, *, n_experts):
    i = pl.program_id(0)

    @pl.when(i == 0)
    def _():
        carry_ref[...] = jnp.zeros_like(carry_ref)

    tm = lg_ref.shape[0]
    logits = lg_ref[...]
    lane = lax.broadcasted_iota(jnp.int32, logits.shape, 1)
    neg = jnp.float32(-jnp.inf)
    lg = jnp.where(lane < n_experts, logits, neg)
    v1 = jnp.max(lg, axis=-1, keepdims=True)
    i1 = jnp.min(jnp.where(lg == v1, lane, LANES), axis=-1, keepdims=True)
    lg2 = jnp.where(lane == i1, neg, lg)
    v2 = jnp.max(lg2, axis=-1, keepdims=True)
    i2 = jnp.min(jnp.where(lg2 == v2, lane, LANES), axis=-1, keepdims=True)
    e = jnp.exp(v2 - v1)
    w1 = 1.0 / (1.0 + e)
    w2 = e / (1.0 + e)
    sel1 = lane == i1
    sel2 = lane == i2
    onehot = jnp.where(jnp.logical_or(sel1, sel2), 1.0, 0.0)
    rr = lax.broadcasted_iota(jnp.int32, (tm, tm), 0)
    cc = lax.broadcasted_iota(jnp.int32, (tm, tm), 1)
    tri = jnp.where(rr > cc, 1.0, 0.0).astype(BF16)
    before = jnp.dot(tri, onehot.astype(BF16), preferred_element_type=F32) + carry_ref[...]
    r1 = jnp.sum(jnp.where(sel1, before, 0.0), axis=-1, keepdims=True).astype(jnp.int32)
    r2 = jnp.sum(jnp.where(sel2, before, 0.0), axis=-1, keepdims=True).astype(jnp.int32)
    carry_ref[...] = carry_ref[...] + jnp.sum(onehot, axis=0, keepdims=True)
    meta_ref[...] = jnp.where(lane == 0, i1, jnp.where(lane == 1, i2, jnp.where(lane == 2, r1,
                              jnp.where(lane == 3, r2, 0))))
    wts_ref[...] = jnp.where(lane == 0, w1, jnp.where(lane == 1, w2, 0.0))
    cnt_ref[...] = jnp.broadcast_to(carry_ref[...], cnt_ref.shape)


def _route(cfg, logits, n_experts):
    n = logits.shape[0]
    tm = cfg["tm_rows"]
    return pl.pallas_call(
        functools.partial(_route_kernel, n_experts=n_experts),
        grid=(n // tm,),
        in_specs=[pl.BlockSpec((tm, LANES), lambda i: (i, 0))],
        out_specs=[pl.BlockSpec((tm, LANES), lambda i: (i, 0)), pl.BlockSpec((tm, LANES), lambda i: (i, 0)),
                   pl.BlockSpec((SUBLANES, LANES), lambda i: (0, 0))],
        out_shape=[jax.ShapeDtypeStruct((n, LANES), jnp.int32), jax.ShapeDtypeStruct((n, LANES), F32),
                   jax.ShapeDtypeStruct((SUBLANES, LANES), F32)],
        scratch_shapes=[pltpu.VMEM((1, LANES), F32)],
        compiler_params=_params(("arbitrary",)),
        name="route",
    )(logits)


GATHER_PITCH_PAD = SUBLANES


def _slab_gather_kernel(idx0_ref, idxn_ref, src_ref, *rest, rows, per_row, nsub, steps, weighted):
    if weighted:
        w_ref, o_ref, buf, sems = rest
    else:
        o_ref, buf, sems = rest
    i = pl.program_id(0)
    slot = i % 2
    pitch = nsub + GATHER_PITCH_PAD
    n_slabs = per_row * rows
    unroll = 8

    def slab_copy(src_slab, dst_slab, sl):
        src0 = pl.multiple_of(src_slab * nsub, nsub)
        dst0 = pl.multiple_of(dst_slab * pitch, SUBLANES)
        return pltpu.make_async_copy(src_ref.at[pl.ds(src0, nsub)], buf.at[sl, pl.ds(dst0, nsub)], sems.at[sl])

    def issue(idx_ref, sl):
        def body(o, carry):
            for u in range(unroll):
                r = o * unroll + u
                slab_copy(idx_ref[0, r], r, sl).start(priority=u % 2)
            return carry

        lax.fori_loop(0, n_slabs // unroll, body, 0)

    def drain(sl):
        def body(o, carry):
            for u in range(unroll):
                slab_copy(0, 0, sl).wait()
            return carry

        lax.fori_loop(0, n_slabs // unroll, body, 0)

    @pl.when(i == 0)
    def _():
        issue(idx0_ref, 0)

    @pl.when(i + 1 < steps)
    def _():
        issue(idxn_ref, 1 - slot)

    drain(slot)

    if weighted:
        wk = [jnp.broadcast_to(w_ref[:, k:k + 1], (rows, LANES)) for k in range(per_row)]

    for sl in range(2):
        @pl.when(slot == sl)
        def _():
            for s in range(nsub):
                parts = [buf[sl, pl.ds(k * rows * pitch + s, rows, stride=pitch), :] for k in range(per_row)]
                if weighted:
                    v = wk[0] * parts[0]
                    for k in range(1, per_row):
                        v = v + wk[k] * parts[k]
                else:
                    (v,) = parts
                o_ref[:, s * LANES:(s + 1) * LANES] = v.astype(o_ref.dtype)


def _slab_gather(src_slabs, idx, *, d, rows, per_row=1, wts=None, out_dtype=F32, name="slab_gather"):
    n_out = idx.shape[1]
    nsub = d // LANES
    steps = n_out // rows
    pitch = nsub + GATHER_PITCH_PAD
    idx3 = idx.reshape(per_row, steps, rows).transpose(1, 0, 2).reshape(steps, 1, per_row * rows)
    smem_idx = lambda imap: pl.BlockSpec((None, 1, per_row * rows), imap, memory_space=pltpu.SMEM)
    in_specs = [smem_idx(lambda i: (0, 0, 0)),
                smem_idx(lambda i: (jnp.minimum(i + 1, steps - 1), 0, 0)),
                pl.BlockSpec(memory_space=pl.ANY)]
    args = [idx3, idx3, src_slabs]
    if wts is not None:
        in_specs.append(pl.BlockSpec((rows, LANES), lambda i: (i, 0)))
        args.append(wts)
    return pl.pallas_call(
        functools.partial(_slab_gather_kernel, rows=rows, per_row=per_row, nsub=nsub, steps=steps,
                          weighted=wts is not None),
        grid=(steps,),
        in_specs=in_specs,
        out_specs=pl.BlockSpec((rows, d), lambda i: (i, 0)),
        out_shape=jax.ShapeDtypeStruct((n_out, d), out_dtype),
        scratch_shapes=[pltpu.VMEM((2, per_row * rows * pitch, LANES), F32), pltpu.SemaphoreType.DMA((2,))],
        compiler_params=_params(("arbitrary",)),
        name=name,
    )(*args)


def _moe(cfg, logits, h_slabs, n_exp, layer, w_gu, w_down):
    n = logits.shape[0]
    d = w_gu.shape[1]
    tm = cfg["tm_moe"]
    meta, wts, cnt = _route(cfg, logits, n_exp)
    counts = cnt[0, :n_exp].astype(jnp.int32)
    tiles_e = (counts + tm - 1) // tm
    tiles_cum = jnp.cumsum(tiles_e)
    row_off = (tiles_cum - tiles_e) * tm
    pos = jnp.stack([row_off[meta[:, 0]] + meta[:, 2], row_off[meta[:, 1]] + meta[:, 3]])
    n_tiles = (2 * n) // tm + n_exp
    tok = jnp.arange(n, dtype=jnp.int32)
    src = jnp.zeros((n_tiles * tm,), jnp.int32).at[pos.reshape(-1)].set(jnp.concatenate([tok, tok]))
    n_active = tiles_cum[-1:]
    tile_ids = jnp.minimum(jnp.arange(n_tiles, dtype=jnp.int32), n_active[0] - 1)
    tile_expert = jnp.minimum(jnp.sum(tiles_cum[None, :] <= tile_ids[:, None], axis=1), n_exp - 1).astype(jnp.int32)
    experts = jnp.arange(n_exp, dtype=jnp.int32)
    later = jnp.logical_and(experts[None, :] > experts[:, None], tiles_e[None, :] > 0)
    next_present = jnp.min(jnp.where(later, experts[None, :], n_exp), axis=1)
    next_expert = next_present[tile_expert]
    next_expert = jnp.where(next_expert < n_exp, next_expert + layer * n_exp, -1).astype(jnp.int32)
    tile_expert = tile_expert + layer * n_exp

    xs = _slab_gather(h_slabs, src[None], d=d, rows=cfg["rows_dispatch"], out_dtype=BF16, name="moe_dispatch")
    mid = _swiglu_matmul(xs, w_gu, tile_expert, next_expert, n_active, tm=tm, tn=cfg["tn_ff"], name="moe_up")
    out_slabs = _grouped_matmul(mid, w_down, tile_expert, n_active, tm=tm, tn=d, slab=True, name="moe_down")
    return _slab_gather(out_slabs, pos, d=d, rows=cfg["rows_combine"], per_row=2, wts=wts, name="moe_combine")


def _rope_tables(cfg):
    hd, seq, ctx = cfg["hd"], cfg["SEQ"], cfg["CTX"]
    rows = seq // GRID_W
    r, col = jnp.meshgrid(jnp.arange(rows), jnp.arange(GRID_W), indexing="ij")
    r = r.reshape(-1).astype(F32)
    col = col.reshape(-1).astype(F32)
    n_freq = hd // 4
    inv = ROPE_THETA ** (-jnp.arange(n_freq, dtype=F32) / n_freq)
    ang = jnp.concatenate([r[:, None] * inv, col[:, None] * inv], axis=-1)
    cos = jnp.repeat(jnp.cos(ang), 2, axis=-1)
    sin = jnp.repeat(jnp.sin(ang), 2, axis=-1) * jnp.tile(jnp.array([-1.0, 1.0], F32), hd // 2)
    cos = jnp.concatenate([jnp.ones((ctx, hd), F32), cos], axis=0)
    sin = jnp.concatenate([jnp.zeros((ctx, hd), F32), sin], axis=0)
    return cos, sin


def kernel(x, c, ctx, c_ctx, ada_w, ada_b, norm_g, attn_w_qkv, attn_q_norm, attn_k_norm, attn_w_o,
           conv_w_in, conv_w, conv_w_out, ffn_w_gu, ffn_w_down, moe_router, moe_w_gu, moe_w_down):
    nb, seq, d = x.shape
    ctx_len = ctx.shape[1]
    depth = ada_w.shape[0]
    hd = attn_q_norm.shape[1]
    nh = attn_w_o.shape[1] // hd
    kvh = (attn_w_qkv.shape[2] // hd - nh) // 2
    d_ff = ffn_w_down.shape[1]
    t_len = ctx_len + seq
    tm_rows = _pick(ctx_len, (256, 128))

    def stream_cfg(ctx_rows):
        rows = nb * (ctx_rows + seq)
        return dict(B=nb, SEQ=seq, CTX=ctx_rows, T=ctx_rows + seq, hd=hd, NH=nh, KVH=kvh,
                    R=-(-(nb + 1) // SUBLANES) * SUBLANES, tm_rows=tm_rows, tq=_pick(ctx_len, (256, 128)),
                    tm_mm=_pick(rows, (512, 256, 128)), tm_moe=_pick(2 * rows, (512, 256, 128)),
                    tn_ff=_pick(d_ff, (1408, 512, 256, 128)), tn_down=_pick(d, (512, 256, 128)),
                    tn_proj=_pick(d, (1024, 512, 256, 128)),
                    rows_dispatch=256, rows_combine=128)

    cfg = stream_cfg(ctx_len)
    ctx_free_from = depth - 1 if (depth - 1) % 2 == 1 else depth
    n = nb * t_len

    xa = jnp.concatenate([ctx, x], axis=1).reshape(n, d)
    cvec = jnp.zeros((cfg["R"], d), F32).at[:nb].set(c).at[nb].set(c_ctx)
    mod = _modulation(cvec, ada_w, ada_b)
    mod = mod.reshape(depth, cfg["R"], 6, d).transpose(0, 2, 1, 3).reshape(depth * 6 * cfg["R"], 1, d)
    cos_t, sin_t = _rope_tables(cfg)

    w_qkv = attn_w_qkv.astype(BF16)
    w_o = attn_w_o.astype(BF16)
    w_cin = conv_w_in.astype(BF16)
    w_cout = conv_w_out.astype(BF16)
    w_dn = ffn_w_down.astype(BF16)
    n_exp = moe_router.shape[2]
    m_gu = moe_w_gu.reshape((-1,) + moe_w_gu.shape[2:])
    m_dn = moe_w_down.astype(BF16).reshape((-1,) + moe_w_down.shape[2:])

    (h,) = _resnorm(cfg, xa, mod, g_pre=norm_g[0, 0], shift_idx=(0, 0), scale_idx=(0, 1))
    for i in range(depth):
        j = i // 2
        moe_layer = i % 2 == 1
        tm = cfg["tm_mm"]
        if i % 2 == 0:
            q_scale = (hd ** -0.5) * 1.4426950408889634
            gain = jnp.concatenate([jnp.tile(attn_q_norm[j] * q_scale, nh), jnp.tile(attn_k_norm[j], kvh),
                                    jnp.ones((kvh * hd,), F32)]).reshape(1, -1)
            qkv = _qkv_proj(cfg, h, w_qkv[j], gain, cos_t, sin_t)
            o = _attention(cfg, qkv)
            y = _dense_matmul(o, w_o, group=j, tm=tm, tn=cfg["tn_proj"], name="attn_out")
        else:
            z = _dense_matmul(h, w_cin, group=j, tm=tm, tn=_pick(3 * d, (1024, 512, 256, 128)), name="conv_in")
            gated = _convgate(cfg, z, conv_w[j])
            y = _dense_matmul(gated, w_cout, group=j, tm=tm, tn=cfg["tn_proj"], name="conv_out")
        router = jnp.zeros((d, LANES), F32).at[:, :n_exp].set(moe_router[j]) if moe_layer else None
        res = _resnorm(cfg, xa, mod, y=y, g_post=norm_g[i, 1], gate_idx=(i, 2), g_pre=norm_g[i, 2],
                       shift_idx=(i, 3), scale_idx=(i, 4), router=router)
        xa, h = res[0], res[1]
        last = i == depth - 1
        nxt = {} if last else dict(g_pre=norm_g[i + 1, 0], shift_idx=(i + 1, 0), scale_idx=(i + 1, 1))
        if moe_layer:
            y = _moe(cfg, res[1], res[2], n_exp, j, m_gu, m_dn)
        else:
            mid = _dense_matmul(h, ffn_w_gu, group=j, tm=tm, tn=cfg["tn_ff"], swiglu=True, out_dtype=BF16,
                                name="ffn_up")
            y = _dense_matmul(mid, w_dn, group=j, tm=tm, tn=cfg["tn_down"], name="ffn_down")
        drop_ctx = cfg["CTX"] > 0 and (last or i + 1 >= ctx_free_from)
        res = _resnorm(cfg, xa, mod, y=y, g_post=norm_g[i, 3], gate_idx=(i, 5), latent_only=drop_ctx, **nxt)
        if drop_ctx:
            cfg = stream_cfg(0)
        xa = res[0]
        if not last:
            h = res[1]
    return xa.reshape(nb, seq, d)
```
